```python
import jax, jax.numpy as jnp
from jax import lax
import numpy as np

D_MODEL = 1024
BATCH = 4
SEQ = 8192
DEPTH = 4

HG_HEADS = 8
HG_DK = 128
HG_DV = D_MODEL // HG_HEADS
HG_CHUNK = 64
LB_FLOOR = 1e-30
MLA_HEADS = 8
MLA_NOPE = 128
MLA_ROPE = 64
MLA_V = 128
Q_LORA = 384
KV_LORA = 256
ROPE_THETA = 10000.0
Q_BLOCK = 128
MASK_VALUE = -1e30
D_FF = 2816
DN_ALPHA = (2.0 * DEPTH) ** 0.25
DN_BETA = (8.0 * DEPTH) ** -0.25
LN_EPS = 1e-5
RMS_EPS = 1e-6
N_ADA = 9

SPLITS = (HG_HEADS * HG_DK,
          HG_HEADS * HG_DK,
          HG_HEADS * HG_DV,
          HG_HEADS * HG_DV,
          Q_LORA,
          KV_LORA,
          MLA_ROPE,
          D_MODEL,
          D_MODEL)
D_IN = sum(SPLITS)
SPLIT_POINTS = [int(v) for v in np.cumsum(SPLITS)[:-1]]

kernel_name = "hybrid_hgrn2_mla_macaron_deepnorm_adaln"


def layer_norm(x, g, b):
    xf = x.astype(jnp.float32)
    mu = jnp.mean(xf, -1, keepdims=True)
    var = jnp.mean(jnp.square(xf - mu), -1, keepdims=True)
    return ((xf - mu) * lax.rsqrt(var + LN_EPS)).astype(x.dtype) * g + b


def rms_norm(x, g):
    xf = x.astype(jnp.float32)
    return (xf * lax.rsqrt(jnp.mean(xf * xf, -1, keepdims=True) + RMS_EPS)).astype(x.dtype) * g


def rope_angles(positions):
    inv = 1.0 / (ROPE_THETA ** (jnp.arange(0, MLA_ROPE, 2, dtype=jnp.float32) / MLA_ROPE))
    ang = positions.astype(jnp.float32)[..., None] * inv
    return jnp.cos(ang), jnp.sin(ang)


def apply_rope(x, cos, sin):
    x1, x2 = jnp.split(x, 2, axis=-1)
    cos = cos.astype(x.dtype)
    sin = sin.astype(x.dtype)
    return jnp.concatenate([x1 * cos - x2 * sin, x1 * sin + x2 * cos], axis=-1)


def swiglu(u, w_gate, w_up, w_down):
    return (jax.nn.silu(u @ w_gate) * (u @ w_up)) @ w_down


def hgrn2_recurrence(q, k, v, log_f):
    B, S, H, DK = q.shape
    DV = v.shape[-1]
    n = S // HG_CHUNK

    def chunks(t):
        return t.astype(jnp.float32).reshape(B, n, HG_CHUNK, H, t.shape[-1]).transpose(1, 0, 3, 2, 4)

    qc, kc, vc = chunks(q), chunks(k), chunks(v)
    G = jnp.cumsum(chunks(log_f), axis=3)
    causal = jnp.tril(jnp.ones((HG_CHUNK, HG_CHUNK), bool))[:, :, None]

    def step(state, inp):
        qb, kb, vb, Gb = inp
        diff = Gb[:, :, :, None, :] - Gb[:, :, None, :, :]
        decay = jnp.where(causal, jnp.exp(jnp.where(causal, diff, 0.0)), 0.0)
        scores = jnp.einsum('bhtd,bhsd,bhtsd->bhts', qb, kb, decay)
        o = (jnp.einsum('bhts,bhse->bhte', scores, vb)
             + jnp.einsum('bhtd,bhde->bhte', qb * jnp.exp(Gb), state))
        G_end = Gb[:, :, -1:, :]
        state = (jnp.exp(G_end[:, :, 0, :])[..., None] * state
                 + jnp.einsum('bhsd,bhse->bhde', kb * jnp.exp(G_end - Gb), vb))
        return state, o

    state0 = jnp.zeros((B, H, DK, DV), jnp.float32)
    _, o = lax.scan(step, state0, (qc, kc, vc, G))
    return o.transpose(1, 0, 3, 2, 4).reshape(B, S, H, DV).astype(v.dtype)


def causal_block_attention(q, k, v):
    S = q.shape[1]
    scale = (MLA_NOPE + MLA_ROPE) ** -0.5
    outs = []
    for blk in range(S // Q_BLOCK):
        q0 = blk * Q_BLOCK
        kend = q0 + Q_BLOCK
        s = jnp.einsum('bqhd,bkhd->bhqk', q[:, q0:kend], k[:, :kend]).astype(jnp.float32) * scale
        mask = (q0 + jnp.arange(Q_BLOCK))[:, None] >= jnp.arange(kend)[None, :]
        p = jax.nn.softmax(jnp.where(mask, s, MASK_VALUE), axis=-1).astype(v.dtype)
        outs.append(jnp.einsum('bhqk,bkhd->bqhd', p, v[:, :kend]))
    return jnp.concatenate(outs, axis=1)


def token_mixer(u, cos, sin, lb, w_in, hg_norm_g, q_norm_g, w_uq, kv_norm_g, w_ukv,
                w_branch_hg, w_branch_mla, w_out):
    B, S, _ = u.shape
    zq, zf, zi, zg, cq, ckv, kr, ga, gb = jnp.split(u @ w_in, SPLIT_POINTS, axis=-1)

    lbh = lb.reshape(HG_HEADS, HG_DK)
    zf32 = zf.astype(jnp.float32).reshape(B, S, HG_HEADS, HG_DK)
    log_f = jnp.logaddexp(jnp.log(jnp.maximum(lbh, LB_FLOOR)),
                          jnp.log1p(-lbh) + jax.nn.log_sigmoid(zf32))
    hk = ((1.0 - lbh) * jax.nn.sigmoid(-zf32)).astype(u.dtype)
    hq = jax.nn.silu(zq).reshape(B, S, HG_HEADS, HG_DK)
    hv = zi.reshape(B, S, HG_HEADS, HG_DV)
    o_hg = hgrn2_recurrence(hq, hk, hv, log_f)
    o_hg = rms_norm(o_hg, hg_norm_g) * jax.nn.silu(zg).reshape(B, S, HG_HEADS, HG_DV)
    y_hg = o_hg.reshape(B, S, HG_HEADS * HG_DV) @ w_branch_hg

    q = (rms_norm(cq, q_norm_g) @ w_uq).reshape(B, S, MLA_HEADS, MLA_NOPE + MLA_ROPE)
    q = jnp.concatenate([q[..., :MLA_NOPE],
                         apply_rope(q[..., MLA_NOPE:], cos[:, :, None], sin[:, :, None])], axis=-1)
    kv = (rms_norm(ckv, kv_norm_g) @ w_ukv).reshape(B, S, MLA_HEADS, MLA_NOPE + MLA_V)
    k_rope = apply_rope(kr, cos, sin)[:, :, None, :]
    k = jnp.concatenate([kv[..., :MLA_NOPE],
                         jnp.broadcast_to(k_rope, (B, S, MLA_HEADS, MLA_ROPE))], axis=-1)
    v = kv[..., MLA_NOPE:]
    o_mla = causal_block_attention(q, k, v)
    y_mla = o_mla.reshape(B, S, MLA_HEADS * MLA_V) @ w_branch_mla

    merged = jax.nn.sigmoid(ga) * y_hg + jax.nn.sigmoid(gb) * y_mla
    return merged @ w_out


def setup_inputs(seed: int = 0) -> dict:
    key = jax.random.key(seed)
    ks = iter(jax.random.split(key, 32))

    def nrm(shape, fan_in, gain=1.0):
        return jax.random.normal(next(ks), shape, jnp.float32) * (gain * fan_in ** -0.5)

    def gains(shape):
        return 1.0 + 0.02 * jax.random.normal(next(ks), shape, jnp.float32)

    def small(shape, s=0.02):
        return s * jax.random.normal(next(ks), shape, jnp.float32)

    x = jax.random.normal(next(ks), (BATCH, SEQ, D_MODEL), jnp.float32)
    c = jax.random.normal(next(ks), (BATCH, D_MODEL), jnp.float32)
    offset = jax.random.randint(next(ks), (BATCH, 1), 0, 4096)
    positions = (jnp.arange(SEQ, dtype=jnp.int32)[None, :] + offset).astype(jnp.int32)
    return {
        "x": x,
        "c": c,
        "positions": positions,
        "ada_w": nrm((DEPTH, D_MODEL, N_ADA * D_MODEL), D_MODEL, 0.2),
        "ada_b": small((DEPTH, N_ADA * D_MODEL)),
        "ln_g": gains((DEPTH, 3, D_MODEL)),
        "ln_b": small((DEPTH, 3, D_MODEL)),
        "ffn1_gate": nrm((DEPTH, D_MODEL, D_FF), D_MODEL),
        "ffn1_up": nrm((DEPTH, D_MODEL, D_FF), D_MODEL),
        "ffn1_down": nrm((DEPTH, D_FF, D_MODEL), D_FF, DN_BETA),
        "w_in": nrm((DEPTH, D_MODEL, D_IN), D_MODEL),
        "hg_lower_bound": small((DEPTH, HG_HEADS * HG_DK), 0.1),
        "hg_norm_g": gains((DEPTH, HG_HEADS, HG_DV)),
        "mla_q_norm_g": gains((DEPTH, Q_LORA)),
        "mla_w_uq": nrm((DEPTH, Q_LORA, MLA_HEADS * (MLA_NOPE + MLA_ROPE)), Q_LORA),
        "mla_kv_norm_g": gains((DEPTH, KV_LORA)),
        "mla_w_ukv": nrm((DEPTH, KV_LORA, MLA_HEADS * (MLA_NOPE + MLA_V)), KV_LORA),
        "w_branch_hg": nrm((DEPTH, HG_HEADS * HG_DV, D_MODEL), HG_HEADS * HG_DV),
        "w_branch_mla": nrm((DEPTH, MLA_HEADS * MLA_V, D_MODEL), MLA_HEADS * MLA_V),
        "w_out": nrm((DEPTH, D_MODEL, D_MODEL), D_MODEL, DN_BETA),
        "ffn2_gate": nrm((DEPTH, D_MODEL, D_FF), D_MODEL),
        "ffn2_up": nrm((DEPTH, D_MODEL, D_FF), D_MODEL),
        "ffn2_down": nrm((DEPTH, D_FF, D_MODEL), D_FF, DN_BETA),
    }


def reference(x, c, positions, ada_w, ada_b, ln_g, ln_b, ffn1_gate, ffn1_up, ffn1_down, w_in,
              hg_lower_bound, hg_norm_g, mla_q_norm_g, mla_w_uq, mla_kv_norm_g, mla_w_ukv,
              w_branch_hg, w_branch_mla, w_out, ffn2_gate, ffn2_up, ffn2_down):
    B = x.shape[0]
    cos, sin = rope_angles(positions)
    lb_soft = jax.nn.softmax(hg_lower_bound.astype(jnp.float32), axis=0)
    lower = jnp.cumsum(lb_soft, axis=0) - lb_soft[0]
    cond = jax.nn.silu(c)
    h = x
    for l in range(DEPTH):
        ada = (cond @ ada_w[l] + ada_b[l]).reshape(B, 1, N_ADA, D_MODEL)

        u = h * (1.0 + ada[:, :, 1]) + ada[:, :, 0]
        y = swiglu(u, ffn1_gate[l], ffn1_up[l], ffn1_down[l])
        h = layer_norm(DN_ALPHA * h + 0.5 * (1.0 + ada[:, :, 2]) * y, ln_g[l, 0], ln_b[l, 0])

        u = h * (1.0 + ada[:, :, 4]) + ada[:, :, 3]
        y = token_mixer(u, cos, sin, lower[l], w_in[l], hg_norm_g[l], mla_q_norm_g[l], mla_w_uq[l],
                        mla_kv_norm_g[l], mla_w_ukv[l], w_branch_hg[l], w_branch_mla[l], w_out[l])
        h = layer_norm(DN_ALPHA * h + (1.0 + ada[:, :, 5]) * y, ln_g[l, 1], ln_b[l, 1])

        u = h * (1.0 + ada[:, :, 7]) + ada[:, :, 6]
        y = swiglu(u, ffn2_gate[l], ffn2_up[l], ffn2_down[l])
        h = layer_norm(DN_ALPHA * h + 0.5 * (1.0 + ada[:, :, 8]) * y, ln_g[l, 2], ln_b[l, 2])
    return h
```

```python
import functools

import jax
import jax.numpy as jnp
import numpy as np
from jax import lax
from jax.experimental import pallas as pl
from jax.experimental.pallas import tpu as pltpu

F32 = jnp.float32
BF16 = jnp.bfloat16

D_MODEL = 1024
DEPTH = 4
HEADS = 8
HG_DK = 128
HG_DV = 128
HG_CHUNK = 64
LB_FLOOR = 1e-30
MLA_NOPE = 128
MLA_ROPE = 64
MLA_QK = MLA_NOPE + MLA_ROPE
MLA_V = 128
Q_LORA = 384
KV_LORA = 256
ROPE_THETA = 10000.0
MASK_VALUE = -1e30
D_FF = 2816
DN_ALPHA = (2.0 * DEPTH) ** 0.25
LN_EPS = 1e-5
RMS_EPS = 1e-6
N_ADA = 9
ATTN_SCALE = float(MLA_QK) ** -0.5

VMEM_LIMIT_BYTES = 56 * 1024 * 1024


def _params(*sem):
    return pltpu.CompilerParams(dimension_semantics=sem, vmem_limit_bytes=VMEM_LIMIT_BYTES)


def _resident(shape):
    zeros = (0,) * len(shape)
    return pl.BlockSpec(shape, lambda *_: zeros, pipeline_mode=pl.Buffered(1))


def _silu(x):
    return x * jax.nn.sigmoid(x)


def _dot(a, b):
    return jnp.dot(a, b, preferred_element_type=F32)


def _dot_nt(a, b):
    return lax.dot_general(a, b, (((1,), (1,)), ((), ())), preferred_element_type=F32)


def _dot_tn(a, b):
    return lax.dot_general(a, b, (((0,), (0,)), ((), ())), preferred_element_type=F32)


def _layer_norm(r, g, b):
    mu = jnp.mean(r, axis=-1, keepdims=True)
    d = r - mu
    var = jnp.mean(d * d, axis=-1, keepdims=True)
    return d * lax.rsqrt(var + LN_EPS) * g + b


def _rms_norm(x, g):
    return x * lax.rsqrt(jnp.mean(x * x, axis=-1, keepdims=True) + RMS_EPS) * g


def _ada_kernel(c_ref, w_ref, b_ref, o_ref):
    cond = _silu(c_ref[...])
    o_ref[0] = jnp.dot(cond, w_ref[0], preferred_element_type=F32,
                       precision=lax.Precision.HIGHEST) + b_ref[0]


def _ada_call(c, ada_w, ada_b):
    batch = c.shape[0]
    n_out = ada_w.shape[-1]
    tn = D_MODEL
    return pl.pallas_call(
        _ada_kernel,
        grid=(DEPTH, n_out // tn),
        in_specs=[
            pl.BlockSpec((batch, D_MODEL), lambda l, j: (0, 0)),
            pl.BlockSpec((1, D_MODEL, tn), lambda l, j: (l, 0, j)),
            pl.BlockSpec((1, 1, tn), lambda l, j: (l, 0, j)),
        ],
        out_specs=pl.BlockSpec((1, batch, tn), lambda l, j: (l, 0, j)),
        out_shape=jax.ShapeDtypeStruct((DEPTH, batch, n_out), F32),
        compiler_params=_params("arbitrary", "arbitrary"),
        name="ada",
    )(c, ada_w, ada_b.reshape(DEPTH, 1, n_out))


def _rope_kernel(pos_ref, inv_ref, sign_ref, cos_ref, sin_ref):
    ang = pos_ref[...].astype(F32) * inv_ref[...]
    cos_ref[...] = jnp.cos(ang)
    sin_ref[...] = jnp.sin(ang) * sign_ref[...]


def _rope_call(positions):
    tokens = positions.size
    tm = min(1024, tokens)
    half = MLA_ROPE // 2
    inv = 1.0 / (ROPE_THETA ** (jnp.arange(0, MLA_ROPE, 2, dtype=F32) / MLA_ROPE))
    inv_t = jnp.tile(inv, 4).reshape(1, 4 * half)
    sign = jnp.tile(jnp.concatenate([-jnp.ones((half,), F32), jnp.ones((half,), F32)]), 2)
    sign = sign.reshape(1, 4 * half)
    row = pl.BlockSpec((tm, 4 * half), lambda i: (i, 0))
    const = pl.BlockSpec((1, 4 * half), lambda i: (0, 0))
    return pl.pallas_call(
        _rope_kernel,
        grid=(tokens // tm,),
        in_specs=[pl.BlockSpec((tm, 1), lambda i: (i, 0)), const, const],
        out_specs=[row, row],
        out_shape=[jax.ShapeDtypeStruct((tokens, 4 * half), F32)] * 2,
        compiler_params=_params("arbitrary"),
        name="rope_tables",
    )(positions.reshape(tokens, 1), inv_t, sign)


def _ffn_kernel(h_ref, ada_ref, wg_ref, wu_ref, wd_ref, lng_ref, lnb_ref, o_ref, *, k0):
    h = h_ref[...]
    shift = ada_ref[0, k0:k0 + 1, :]
    scale = ada_ref[0, k0 + 1:k0 + 2, :]
    gate = ada_ref[0, k0 + 2:k0 + 3, :]
    u = (h * (1.0 + scale) + shift).astype(BF16)
    g = _dot(u, wg_ref[...])
    up = _dot(u, wu_ref[...])
    a = (_silu(g) * up).astype(BF16)
    y = _dot(a, wd_ref[...])
    r = DN_ALPHA * h + (0.5 * (1.0 + gate)) * y
    o_ref[...] = _layer_norm(r, lng_ref[...], lnb_ref[...])


def _ffn_call(h, ada_l, wg, wu, wd, lng, lnb, *, k0, seq, tm):
    tokens = h.shape[0]
    per_b = seq // tm
    row = pl.BlockSpec((tm, D_MODEL), lambda i: (i, 0))
    return pl.pallas_call(
        functools.partial(_ffn_kernel, k0=k0),
        grid=(tokens // tm,),
        in_specs=[
            row,
            pl.BlockSpec((1, N_ADA, D_MODEL), lambda i: (i // per_b, 0, 0)),
            _resident((D_MODEL, D_FF)),
            _resident((D_MODEL, D_FF)),
            _resident((D_FF, D_MODEL)),
            _resident((1, D_MODEL)),
            _resident((1, D_MODEL)),
        ],
        out_specs=row,
        out_shape=jax.ShapeDtypeStruct((tokens, D_MODEL), F32),
        compiler_params=_params("arbitrary"),
        name="ffn",
    )(h, ada_l, wg, wu, wd, lng, lnb)


def _inproj_kernel(h_ref, ada_ref, whg_ref, wlat_ref, wgate_ref, lbraw_ref, qng_ref, wuq_ref,
                   kvng_ref, wukv_ref, cos_ref, sin_ref,
                   hq_ref, lf_ref, hk_ref, hv_ref, og_ref, q_ref, k_ref, v_ref, sga_ref, sgb_ref,
                   *, layer):
    h = h_ref[...]
    shift = ada_ref[0, 3:4, :]
    scale = ada_ref[0, 4:5, :]
    u = (h * (1.0 + scale) + shift).astype(BF16)

    raw = lbraw_ref[...]
    e = jnp.exp(raw - jnp.max(raw, axis=0, keepdims=True))
    lb = jnp.zeros((1, HEADS * HG_DK), F32)
    for i in range(1, layer + 1):
        lb = lb + e[i:i + 1, :]
    lb = lb / jnp.sum(e, axis=0, keepdims=True)
    one_minus_lb = 1.0 - lb

    n = HEADS * HG_DK
    zq = _dot(u, whg_ref[:, 0:n])
    hq_ref[...] = _silu(zq).astype(BF16)

    zf = _dot(u, whg_ref[:, n:2 * n])
    t = jnp.exp(-jnp.abs(zf))
    r = 1.0 / (1.0 + t)
    tr = t * r
    pos = zf >= 0.0
    sig_pos = jnp.where(pos, r, tr)
    sig_neg = jnp.where(pos, tr, r)
    lf_ref[...] = jnp.log(jnp.maximum(lb, LB_FLOOR) + one_minus_lb * sig_pos)
    hk_ref[...] = (one_minus_lb * sig_neg).astype(BF16)

    hv_ref[...] = _dot(u, whg_ref[:, 2 * n:3 * n]).astype(BF16)
    og_ref[...] = _silu(_dot(u, whg_ref[:, 3 * n:4 * n])).astype(og_ref.dtype)

    zl = _dot(u, wlat_ref[...])
    cos_t = cos_ref[...]
    sin_t = sin_ref[...]
    cqn = _rms_norm(zl[:, 0:Q_LORA], qng_ref[...]).astype(BF16)
    qf = _dot(cqn, wuq_ref[...]) * ATTN_SCALE
    nq = HEADS * MLA_NOPE
    nr = HEADS * MLA_ROPE
    cos_q = jnp.concatenate([cos_t] * (nr // 128), axis=1)
    sin_q = jnp.concatenate([sin_t] * (nr // 128), axis=1)
    q_rope = qf[:, nq:nq + nr] * cos_q + qf[:, nq + nr:nq + 2 * nr] * sin_q
    ckvn = _rms_norm(zl[:, Q_LORA:Q_LORA + KV_LORA], kvng_ref[...]).astype(BF16)
    kv = _dot(ckvn, wukv_ref[...])
    c0 = Q_LORA + KV_LORA
    k_rope = (zl[:, c0:c0 + 128] * cos_t + zl[:, c0 + 128:c0 + 256] * sin_t)[:, 0:MLA_ROPE]
    k_rope = k_rope.astype(BF16)
    for hh in range(HEADS):
        q_ref[hh, :, 0:MLA_NOPE] = qf[:, hh * MLA_NOPE:(hh + 1) * MLA_NOPE].astype(BF16)
        q_ref[hh, :, MLA_NOPE:MLA_QK] = q_rope[:, hh * MLA_ROPE:(hh + 1) * MLA_ROPE].astype(BF16)
        k_ref[hh, :, 0:MLA_NOPE] = kv[:, hh * MLA_NOPE:(hh + 1) * MLA_NOPE].astype(BF16)
        k_ref[hh, :, MLA_NOPE:MLA_QK] = k_rope
    v_ref[...] = kv[:, nq:nq + HEADS * MLA_V].astype(BF16)

    zgate = _dot(u, wgate_ref[...])
    sga_ref[...] = jax.nn.sigmoid(zgate[:, 0:D_MODEL]).astype(sga_ref.dtype)
    sgb_ref[...] = jax.nn.sigmoid(zgate[:, D_MODEL:2 * D_MODEL]).astype(sgb_ref.dtype)


def _inproj_call(h, ada_l, w, lbraw, cos_t, sin_t, *, layer, seq, tm):
    tokens = h.shape[0]
    per_b = seq // tm
    row = pl.BlockSpec((tm, D_MODEL), lambda i: (i, 0))
    tab = pl.BlockSpec((tm, 128), lambda i: (i, 0))
    headed = pl.BlockSpec((HEADS, tm, MLA_QK), lambda i: (0, i, 0))
    wide = jax.ShapeDtypeStruct((tokens, D_MODEL), BF16)
    qk = jax.ShapeDtypeStruct((HEADS, tokens, MLA_QK), BF16)
    return pl.pallas_call(
        functools.partial(_inproj_kernel, layer=layer),
        grid=(tokens // tm,),
        in_specs=[
            row,
            pl.BlockSpec((1, N_ADA, D_MODEL), lambda i: (i // per_b, 0, 0)),
            _resident(w["w_hg"].shape),
            _resident(w["w_lat"].shape),
            _resident(w["w_gate"].shape),
            _resident(lbraw.shape),
            _resident((1, Q_LORA)),
            _resident(w["w_uq"].shape),
            _resident((1, KV_LORA)),
            _resident(w["w_ukv"].shape),
            tab, tab,
        ],
        out_specs=[row, row, row, row, row, headed, headed, row, row, row],
        out_shape=[wide, jax.ShapeDtypeStruct((tokens, D_MODEL), F32), wide, wide, wide,
                   qk, qk, wide, wide, wide],
        compiler_params=_params("arbitrary"),
        name="inproj",
    )(h, ada_l, w["w_hg"], w["w_lat"], w["w_gate"], lbraw, w["q_norm_g"], w["w_uq"],
      w["kv_norm_g"], w["w_ukv"], cos_t, sin_t)


def _hgrn_masks():
    t = np.arange(HG_CHUNK)[:, None]
    s = np.arange(HG_CHUNK)[None, :]
    masks = [(t == s)]
    m = 1
    while m < HG_CHUNK:
        masks.append((t // (2 * m) == s // (2 * m)) & ((t & m) != 0) & ((s & m) == 0))
        m *= 2
    return np.stack(masks).astype(np.float32)


def _hgrn_kernel(q_ref, lf_ref, k_ref, v_ref, og_ref, gn_ref, mask_ref, o_ref, state_ref, *, n_chunks):
    @pl.when(pl.program_id(2) == 0)
    def _():
        state_ref[...] = jnp.zeros_like(state_ref)

    row = lax.broadcasted_iota(jnp.int32, (HG_CHUNK, HG_DK), 0)
    gn = gn_ref[0]

    def chunk(c, carry):
        r0 = pl.multiple_of(c * HG_CHUNK, HG_CHUNK)
        rows = pl.ds(r0, HG_CHUNK)
        q = q_ref[rows, :].astype(F32)
        k = k_ref[rows, :].astype(F32)
        v = v_ref[rows, :]
        g = lf_ref[rows, :]
        sh = 1
        while sh < HG_CHUNK:
            g = g + jnp.where(row >= sh, pltpu.roll(g, sh, axis=0), 0.0)
            sh *= 2

        state = state_ref[...]
        o = _dot_nt((q * jnp.exp(g)).astype(BF16), state.astype(BF16))

        scores = _dot_nt(q.astype(BF16), k.astype(BF16)) * mask_ref[0]
        last = g
        m = 1
        lvl = 1
        while m < HG_CHUNK:
            second = (row & m) != 0
            gref = jnp.where(second, pltpu.roll(last, m, axis=0), last)
            last = jnp.where(second, last, pltpu.roll(last, HG_CHUNK - m, axis=0))
            e = jnp.exp(-jnp.abs(g - gref))
            scores = scores + _dot_nt((q * e).astype(BF16), (k * e).astype(BF16)) * mask_ref[lvl]
            m *= 2
            lvl += 1
        o = o + _dot(scores.astype(BF16), v)

        k_dec = (k * jnp.exp(last - g)).astype(BF16)
        state_ref[...] = jnp.exp(last[0:1, :]) * state + _dot_tn(v, k_dec)

        y = _rms_norm(o, gn) * og_ref[rows, :].astype(F32)
        o_ref[rows, :] = y.astype(o_ref.dtype)
        return carry

    lax.fori_loop(0, n_chunks, chunk, 0)


def _hgrn_call(hq, lf, hk, hv, og, gnorm, *, batch, seq, tb):
    tokens = hq.shape[0]
    per_b = seq // tb
    blk = pl.BlockSpec((tb, HG_DK), lambda b, h, i: (b * per_b + i, h))
    masks = jnp.asarray(_hgrn_masks())
    return pl.pallas_call(
        functools.partial(_hgrn_kernel, n_chunks=tb // HG_CHUNK),
        grid=(batch, HEADS, per_b),
        in_specs=[blk, blk, blk, blk, blk,
                  pl.BlockSpec((1, 1, HG_DV), lambda b, h, i: (h, 0, 0)),
                  pl.BlockSpec(masks.shape, lambda b, h, i: (0, 0, 0))],
        out_specs=blk,
        out_shape=jax.ShapeDtypeStruct((tokens, HEADS * HG_DV), BF16),
        scratch_shapes=[pltpu.VMEM((HG_DV, HG_DK), F32)],
        compiler_params=_params("arbitrary", "arbitrary", "arbitrary"),
        name="hgrn",
    )(hq, lf, hk, hv, og, gnorm.reshape(HEADS, 1, HG_DV), masks)


def _attn_kernel(q_ref, k_ref, v_ref, o_ref, m_ref, l_ref, acc_ref, *, tq):
    qi = pl.program_id(2)
    q = q_ref[0]
    m_ref[...] = jnp.full_like(m_ref, -jnp.inf)
    l_ref[...] = jnp.zeros_like(l_ref)
    acc_ref[...] = jnp.zeros_like(acc_ref)

    def update(j, causal):
        rows = pl.ds(pl.multiple_of(j * tq, tq), tq)
        s = _dot_nt(q, k_ref[0, rows, :])
        if causal:
            r = lax.broadcasted_iota(jnp.int32, (tq, tq), 0)
            c = lax.broadcasted_iota(jnp.int32, (tq, tq), 1)
            s = jnp.where(r >= c, s, MASK_VALUE)
        m_old = m_ref[...]
        m_new = jnp.maximum(m_old, jnp.max(s, axis=-1, keepdims=True))
        p = jnp.exp(s - m_new)
        alpha = jnp.exp(m_old - m_new)
        l_ref[...] = alpha * l_ref[...] + jnp.sum(p, axis=-1, keepdims=True)
        acc_ref[...] = alpha * acc_ref[...] + _dot(p.astype(BF16), v_ref[rows, :])
        m_ref[...] = m_new

    def body(j, carry):
        update(j, False)
        return carry

    lax.fori_loop(0, qi, body, 0)
    update(qi, True)
    o_ref[...] = (acc_ref[...] / l_ref[...]).astype(o_ref.dtype)


def _attn_call(q, k, v, *, batch, seq, tq):
    tokens = v.shape[0]
    nq = seq // tq
    return pl.pallas_call(
        functools.partial(_attn_kernel, tq=tq),
        grid=(batch, HEADS, nq),
        in_specs=[
            pl.BlockSpec((1, tq, MLA_QK), lambda b, h, i: (h, b * nq + i, 0)),
            pl.BlockSpec((1, seq, MLA_QK), lambda b, h, i: (h, b, 0)),
            pl.BlockSpec((seq, MLA_V), lambda b, h, i: (b, h)),
        ],
        out_specs=pl.BlockSpec((tq, MLA_V), lambda b, h, i: (b * nq + i, h)),
        out_shape=jax.ShapeDtypeStruct((tokens, HEADS * MLA_V), BF16),
        scratch_shapes=[pltpu.VMEM((tq, 1), F32), pltpu.VMEM((tq, 1), F32),
                        pltpu.VMEM((tq, MLA_V), F32)],
        compiler_params=_params("arbitrary", "arbitrary", "arbitrary"),
        name="mla_attn",
    )(q, k, v)


def _merge_kernel(h_ref, ada_ref, yh_ref, om_ref, sga_ref, sgb_ref, wbh_ref, wbm_ref, wo_ref,
                  lng_ref, lnb_ref, o_ref):
    h = h_ref[...]
    gate = ada_ref[0, 5:6, :]
    y_hg = _dot(yh_ref[...], wbh_ref[...])
    y_mla = _dot(om_ref[...], wbm_ref[...])
    merged = sga_ref[...].astype(F32) * y_hg + sgb_ref[...].astype(F32) * y_mla
    y = _dot(merged.astype(BF16), wo_ref[...])
    r = DN_ALPHA * h + (1.0 + gate) * y
    o_ref[...] = _layer_norm(r, lng_ref[...], lnb_ref[...])


def _merge_call(h, ada_l, yh, om, sga, sgb, wbh, wbm, wo, lng, lnb, *, seq, tm):
    tokens = h.shape[0]
    per_b = seq // tm
    row = pl.BlockSpec((tm, D_MODEL), lambda i: (i, 0))
    sq = _resident((D_MODEL, D_MODEL))
    return pl.pallas_call(
        _merge_kernel,
        grid=(tokens // tm,),
        in_specs=[row, pl.BlockSpec((1, N_ADA, D_MODEL), lambda i: (i // per_b, 0, 0)),
                  row, row, row, row, sq, sq, sq,
                  _resident((1, D_MODEL)), _resident((1, D_MODEL))],
        out_specs=row,
        out_shape=jax.ShapeDtypeStruct((tokens, D_MODEL), F32),
        compiler_params=_params("arbitrary"),
        name="merge_out",
    )(h, ada_l, yh, om, sga, sgb, wbh, wbm, wo, lng, lnb)


def _mixer_weights(w_in, w_uq, w_ukv, q_norm_g, kv_norm_g):
    n = HEADS * HG_DK
    c_cq = 4 * n
    c_ckv = c_cq + Q_LORA
    c_kr = c_ckv + KV_LORA
    c_ga = c_kr + MLA_ROPE
    half = MLA_ROPE // 2
    kr = w_in[:, c_kr:c_ga]
    kr_swapped = jnp.concatenate([kr[:, half:], kr[:, :half]], axis=1)
    pad = jnp.zeros((D_MODEL, 128 - MLA_ROPE), w_in.dtype)
    w_lat = jnp.concatenate([w_in[:, c_cq:c_kr], kr, pad, kr_swapped, pad], axis=1)

    uq = w_uq.reshape(Q_LORA, HEADS, MLA_QK)
    uq_nope = uq[:, :, :MLA_NOPE].reshape(Q_LORA, HEADS * MLA_NOPE)
    uq_rope = uq[:, :, MLA_NOPE:]
    uq_rope_sw = jnp.concatenate([uq_rope[:, :, half:], uq_rope[:, :, :half]], axis=2)
    w_uq_p = jnp.concatenate([uq_nope, uq_rope.reshape(Q_LORA, HEADS * MLA_ROPE),
                              uq_rope_sw.reshape(Q_LORA, HEADS * MLA_ROPE)], axis=1)

    ukv = w_ukv.reshape(KV_LORA, HEADS, MLA_NOPE + MLA_V)
    w_ukv_p = jnp.concatenate([ukv[:, :, :MLA_NOPE].reshape(KV_LORA, HEADS * MLA_NOPE),
                               ukv[:, :, MLA_NOPE:].reshape(KV_LORA, HEADS * MLA_V)], axis=1)
    return {
        "w_hg": w_in[:, :c_cq].astype(BF16),
        "w_lat": w_lat.astype(BF16),
        "w_gate": w_in[:, c_ga:].astype(BF16),
        "w_uq": w_uq_p.astype(BF16),
        "w_ukv": w_ukv_p.astype(BF16),
        "q_norm_g": q_norm_g.reshape(1, Q_LORA),
        "kv_norm_g": kv_norm_g.reshape(1, KV_LORA),
    }


def _tile(seq, want):
    t = min(want, seq)
    assert seq % t == 0, (seq, t)
    return t


def kernel(x, c, positions, ada_w, ada_b, ln_g, ln_b, ffn1_gate, ffn1_up, ffn1_down, w_in,
           hg_lower_bound, hg_norm_g, mla_q_norm_g, mla_w_uq, mla_kv_norm_g, mla_w_ukv,
           w_branch_hg, w_branch_mla, w_out, ffn2_gate, ffn2_up, ffn2_down):
    batch, seq, _ = x.shape
    tokens = batch * seq
    tm_ffn = _tile(seq, 512)
    tm_in = _tile(seq, 256)
    tm_merge = _tile(seq, 512)
    tb_hgrn = _tile(seq, 1024)
    tq = _tile(seq, 256)

    ada = _ada_call(c, ada_w, ada_b).reshape(DEPTH, batch, N_ADA, D_MODEL)
    cos_t, sin_t = _rope_call(positions)
    lbraw = hg_lower_bound.astype(F32)

    h = x.reshape(tokens, D_MODEL)
    for l in range(DEPTH):
        ada_l = ada[l]
        ln = lambda i: (ln_g[l, i].reshape(1, D_MODEL), ln_b[l, i].reshape(1, D_MODEL))

        h = _ffn_call(h, ada_l, ffn1_gate[l].astype(BF16), ffn1_up[l].astype(BF16),
                      ffn1_down[l].astype(BF16), *ln(0), k0=0, seq=seq, tm=tm_ffn)

        w = _mixer_weights(w_in[l], mla_w_uq[l], mla_w_ukv[l], mla_q_norm_g[l], mla_kv_norm_g[l])
        hq, lf, hk, hv, og, q, k, v, sga, sgb = _inproj_call(
            h, ada_l, w, lbraw, cos_t, sin_t, layer=l, seq=seq, tm=tm_in)
        yh = _hgrn_call(hq, lf, hk, hv, og, hg_norm_g[l], batch=batch, seq=seq, tb=tb_hgrn)
        om = _attn_call(q, k, v, batch=batch, seq=seq, tq=tq)
        h = _merge_call(h, ada_l, yh, om, sga, sgb, w_branch_hg[l].astype(BF16),
                        w_branch_mla[l].astype(BF16), w_out[l].astype(BF16), *ln(1),
                        seq=seq, tm=tm_merge)

        h = _ffn_call(h, ada_l, ffn2_gate[l].astype(BF16), ffn2_up[l].astype(BF16),
                      ffn2_down[l].astype(BF16), *ln(2), k0=6, seq=seq, tm=tm_ffn)
    return h.reshape(batch, seq, D_MODEL)
```

```python
import functools

import jax
import jax.numpy as jnp
import numpy as np
from jax import lax
from jax.experimental import pallas as pl
from jax.experimental.pallas import tpu as pltpu

F32 = jnp.float32
BF16 = jnp.bfloat16

D_MODEL = 1024
DEPTH = 4
HEADS = 8
HG_DK = 128
HG_DV = 128
HG_CHUNK = 64
LB_FLOOR = 1e-30
MLA_NOPE = 128
MLA_ROPE = 64
MLA_QK = MLA_NOPE + MLA_ROPE
MLA_V = 128
Q_LORA = 384
KV_LORA = 256
ROPE_THETA = 10000.0
MASK_VALUE = -1e30
D_FF = 2816
DN_ALPHA = (2.0 * DEPTH) ** 0.25
LN_EPS = 1e-5
RMS_EPS = 1e-6
N_ADA = 9
LOG2E = 1.4426950408889634
ATTN_SCALE_LOG2E = float(MLA_QK) ** -0.5 * LOG2E
SUBLANES = 8
ONES_ROWS = 16

VMEM_LIMIT_BYTES = 56 * 1024 * 1024


def _params(*sem):
    return pltpu.CompilerParams(dimension_semantics=sem, vmem_limit_bytes=VMEM_LIMIT_BYTES)


def _resident(shape):
    zeros = (0,) * len(shape)
    return pl.BlockSpec(shape, lambda *_: zeros, pipeline_mode=pl.Buffered(1))


def _silu(x):
    return x * jax.nn.sigmoid(x)


def _dot(a, b):
    return jnp.dot(a, b, preferred_element_type=F32)


def _dot_nt(a, b):
    return lax.dot_general(a, b, (((1,), (1,)), ((), ())), preferred_element_type=F32)


def _dot_tn(a, b):
    return lax.dot_general(a, b, (((0,), (0,)), ((), ())), preferred_element_type=F32)


def _layer_norm(r, g, b):
    mu = jnp.mean(r, axis=-1, keepdims=True)
    d = r - mu
    var = jnp.mean(d * d, axis=-1, keepdims=True)
    return d * lax.rsqrt(var + LN_EPS) * g + b


def _rms_norm(x, g):
    return x * lax.rsqrt(jnp.mean(x * x, axis=-1, keepdims=True) + RMS_EPS) * g


def _ada_kernel(c_ref, w_ref, b_ref, o_ref):
    cond = _silu(c_ref[...])
    o_ref[0] = jnp.dot(cond, w_ref[0], preferred_element_type=F32,
                       precision=lax.Precision.HIGHEST) + b_ref[0]


def _ada_call(c, ada_w, ada_b):
    batch = c.shape[0]
    n_out = ada_w.shape[-1]
    tn = D_MODEL
    return pl.pallas_call(
        _ada_kernel,
        grid=(DEPTH, n_out // tn),
        in_specs=[
            pl.BlockSpec((batch, D_MODEL), lambda l, j: (0, 0)),
            pl.BlockSpec((1, D_MODEL, tn), lambda l, j: (l, 0, j)),
            pl.BlockSpec((1, 1, tn), lambda l, j: (l, 0, j)),
        ],
        out_specs=pl.BlockSpec((1, batch, tn), lambda l, j: (l, 0, j)),
        out_shape=jax.ShapeDtypeStruct((DEPTH, batch, n_out), F32),
        compiler_params=_params("arbitrary", "arbitrary"),
        name="ada",
    )(c, ada_w, ada_b.reshape(DEPTH, 1, n_out))


def _rope_kernel(pos_ref, inv_ref, sign_ref, cos_ref, sin_ref):
    ang = pos_ref[...].astype(F32) * inv_ref[...]
    cos_ref[...] = jnp.cos(ang)
    sin_ref[...] = jnp.sin(ang) * sign_ref[...]


def _rope_call(positions):
    tokens = positions.size
    tm = min(1024, tokens)
    half = MLA_ROPE // 2
    inv = 1.0 / (ROPE_THETA ** (jnp.arange(0, MLA_ROPE, 2, dtype=F32) / MLA_ROPE))
    inv_t = jnp.tile(inv, 4).reshape(1, 4 * half)
    sign = jnp.tile(jnp.concatenate([-jnp.ones((half,), F32), jnp.ones((half,), F32)]), 2)
    sign = sign.reshape(1, 4 * half)
    row = pl.BlockSpec((tm, 4 * half), lambda i: (i, 0))
    const = pl.BlockSpec((1, 4 * half), lambda i: (0, 0))
    return pl.pallas_call(
        _rope_kernel,
        grid=(tokens // tm,),
        in_specs=[pl.BlockSpec((tm, 1), lambda i: (i, 0)), const, const],
        out_specs=[row, row],
        out_shape=[jax.ShapeDtypeStruct((tokens, 4 * half), F32)] * 2,
        compiler_params=_params("arbitrary"),
        name="rope_tables",
    )(positions.reshape(tokens, 1), inv_t, sign)


def _ffn_kernel(h_ref, ada_ref, wg_ref, wu_ref, wd_ref, lng_ref, lnb_ref, o_ref, *, k0):
    h = h_ref[...]
    shift = ada_ref[0, k0:k0 + 1, :]
    scale = ada_ref[0, k0 + 1:k0 + 2, :]
    gate = ada_ref[0, k0 + 2:k0 + 3, :]
    u = (h * (1.0 + scale) + shift).astype(BF16)
    g = _dot(u, wg_ref[...])
    up = _dot(u, wu_ref[...])
    a = (_silu(g) * up).astype(BF16)
    y = _dot(a, wd_ref[...])
    r = DN_ALPHA * h + (0.5 * (1.0 + gate)) * y
    o_ref[...] = _layer_norm(r, lng_ref[...], lnb_ref[...])


def _ffn_call(h, ada_l, wg, wu, wd, lng, lnb, *, k0, seq, tm):
    tokens = h.shape[0]
    per_b = seq // tm
    row = pl.BlockSpec((tm, D_MODEL), lambda i: (i, 0))
    return pl.pallas_call(
        functools.partial(_ffn_kernel, k0=k0),
        grid=(tokens // tm,),
        in_specs=[
            row,
            pl.BlockSpec((1, N_ADA, D_MODEL), lambda i: (i // per_b, 0, 0)),
            _resident((D_MODEL, D_FF)),
            _resident((D_MODEL, D_FF)),
            _resident((D_FF, D_MODEL)),
            _resident((1, D_MODEL)),
            _resident((1, D_MODEL)),
        ],
        out_specs=row,
        out_shape=jax.ShapeDtypeStruct((tokens, D_MODEL), F32),
        compiler_params=_params("arbitrary"),
        name="ffn",
    )(h, ada_l, wg, wu, wd, lng, lnb)


def _inproj_kernel(h_ref, ada_ref, whg_ref, wlat_ref, wgate_ref, lbraw_ref, qng_ref, wuq_ref,
                   kvng_ref, wuk_ref, wvt_ref, cos_ref, sin_ref,
                   hq_ref, lf_ref, hk_ref, hv_ref, og_ref, q_ref, k_ref, vt_ref, sga_ref, sgb_ref,
                   *, layer):
    h = h_ref[...]
    shift = ada_ref[0, 3:4, :]
    scale = ada_ref[0, 4:5, :]
    u = (h * (1.0 + scale) + shift).astype(BF16)

    raw = lbraw_ref[...]
    e = jnp.exp(raw - jnp.max(raw, axis=0, keepdims=True))
    lb = jnp.zeros((1, HEADS * HG_DK), F32)
    for i in range(1, layer + 1):
        lb = lb + e[i:i + 1, :]
    lb = lb / jnp.sum(e, axis=0, keepdims=True)
    one_minus_lb = 1.0 - lb

    n = HEADS * HG_DK
    zq = _dot(u, whg_ref[:, 0:n])
    hq_ref[...] = _silu(zq).astype(BF16)

    zf = _dot(u, whg_ref[:, n:2 * n])
    t = jnp.exp(-jnp.abs(zf))
    r = 1.0 / (1.0 + t)
    tr = t * r
    pos = zf >= 0.0
    sig_pos = jnp.where(pos, r, tr)
    sig_neg = jnp.where(pos, tr, r)
    lf_ref[...] = jnp.log(jnp.maximum(lb, LB_FLOOR) + one_minus_lb * sig_pos)
    hk_ref[...] = (one_minus_lb * sig_neg).astype(BF16)

    hv_ref[...] = _dot(u, whg_ref[:, 2 * n:3 * n]).astype(BF16)
    og_ref[...] = _silu(_dot(u, whg_ref[:, 3 * n:4 * n])).astype(og_ref.dtype)

    zl = _dot(u, wlat_ref[...])
    cos_t = cos_ref[...]
    sin_t = sin_ref[...]
    cqn = _rms_norm(zl[:, 0:Q_LORA], qng_ref[...]).astype(BF16)
    qf = _dot(cqn, wuq_ref[...]) * ATTN_SCALE_LOG2E
    nq = HEADS * MLA_NOPE
    nr = HEADS * MLA_ROPE
    cos_q = jnp.concatenate([cos_t] * (nr // 128), axis=1)
    sin_q = jnp.concatenate([sin_t] * (nr // 128), axis=1)
    q_rope = qf[:, nq:nq + nr] * cos_q + qf[:, nq + nr:nq + 2 * nr] * sin_q
    ckvn = _rms_norm(zl[:, Q_LORA:Q_LORA + KV_LORA], kvng_ref[...]).astype(BF16)
    k_nope = _dot(ckvn, wuk_ref[...])
    vt_ref[...] = _dot_nt(wvt_ref[...], ckvn).astype(BF16)
    c0 = Q_LORA + KV_LORA
    k_rope = (zl[:, c0:c0 + 128] * cos_t + zl[:, c0 + 128:c0 + 256] * sin_t)[:, 0:MLA_ROPE]
    k_rope = k_rope.astype(BF16)
    for hh in range(HEADS):
        q_ref[hh, :, 0:MLA_NOPE] = qf[:, hh * MLA_NOPE:(hh + 1) * MLA_NOPE].astype(BF16)
        q_ref[hh, :, MLA_NOPE:MLA_QK] = q_rope[:, hh * MLA_ROPE:(hh + 1) * MLA_ROPE].astype(BF16)
        k_ref[hh, :, 0:MLA_NOPE] = k_nope[:, hh * MLA_NOPE:(hh + 1) * MLA_NOPE].astype(BF16)
        k_ref[hh, :, MLA_NOPE:MLA_QK] = k_rope

    zgate = _dot(u, wgate_ref[...])
    sga_ref[...] = jax.nn.sigmoid(zgate[:, 0:D_MODEL]).astype(sga_ref.dtype)
    sgb_ref[...] = jax.nn.sigmoid(zgate[:, D_MODEL:2 * D_MODEL]).astype(sgb_ref.dtype)


def _inproj_call(h, ada_l, w, lbraw, cos_t, sin_t, *, layer, seq, tm):
    tokens = h.shape[0]
    per_b = seq // tm
    row = pl.BlockSpec((tm, D_MODEL), lambda i: (i, 0))
    tab = pl.BlockSpec((tm, 128), lambda i: (i, 0))
    headed = pl.BlockSpec((HEADS, tm, MLA_QK), lambda i: (0, i, 0))
    wide = jax.ShapeDtypeStruct((tokens, D_MODEL), BF16)
    qk = jax.ShapeDtypeStruct((HEADS, tokens, MLA_QK), BF16)
    return pl.pallas_call(
        functools.partial(_inproj_kernel, layer=layer),
        grid=(tokens // tm,),
        in_specs=[
            row,
            pl.BlockSpec((1, N_ADA, D_MODEL), lambda i: (i // per_b, 0, 0)),
            _resident(w["w_hg"].shape),
            _resident(w["w_lat"].shape),
            _resident(w["w_gate"].shape),
            _resident(lbraw.shape),
            _resident((1, Q_LORA)),
            _resident(w["w_uq"].shape),
            _resident((1, KV_LORA)),
            _resident(w["w_uk"].shape),
            _resident(w["w_vt"].shape),
            tab, tab,
        ],
        out_specs=[row, row, row, row, row, headed, headed,
                   pl.BlockSpec((HEADS * MLA_V, tm), lambda i: (0, i)), row, row],
        out_shape=[wide, jax.ShapeDtypeStruct((tokens, D_MODEL), F32), wide, wide, wide,
                   qk, qk, jax.ShapeDtypeStruct((HEADS * MLA_V, tokens), BF16), wide, wide],
        compiler_params=_params("arbitrary"),
        name="inproj",
    )(h, ada_l, w["w_hg"], w["w_lat"], w["w_gate"], lbraw, w["q_norm_g"], w["w_uq"],
      w["kv_norm_g"], w["w_uk"], w["w_vt"], cos_t, sin_t)


def _hgrn_masks():
    t = np.arange(HG_CHUNK)[:, None]
    s = np.arange(HG_CHUNK)[None, :]
    masks = [(t == s)]
    m = 1
    while m < HG_CHUNK:
        masks.append((t // (2 * m) == s // (2 * m)) & ((t & m) != 0) & ((s & m) == 0))
        m *= 2
    return np.stack(masks).astype(np.float32)


def _hgrn_kernel(q_ref, lf_ref, k_ref, v_ref, og_ref, gn_ref, mask_ref, o_ref, state_ref, *, n_chunks):
    @pl.when(pl.program_id(1) == 0)
    def _():
        state_ref[...] = jnp.zeros_like(state_ref)

    n_grp = HG_CHUNK // SUBLANES
    sub = lax.broadcasted_iota(jnp.int32, (SUBLANES, HG_DK), 0)
    sign_small = {m: jnp.where((sub & m) != 0, LOG2E, -LOG2E) for m in (1, 2, 4)}

    def bcast_row(x, i):
        return jnp.broadcast_to(x[i:i + 1, :], (SUBLANES, HG_DK))

    def prepare(rows, hh):
        cols = slice(hh * HG_DK, (hh + 1) * HG_DK)
        q = q_ref[rows, cols].astype(F32)
        k = k_ref[rows, cols].astype(F32)

        lf = lf_ref[rows, cols]
        grp = [lf[SUBLANES * r:SUBLANES * (r + 1)] for r in range(n_grp)]
        for sh in (1, 2, 4):
            grp = [p + jnp.where(sub >= sh, pltpu.roll(p, sh, axis=0), 0.0) for p in grp]
        g_grp, ends = [], []
        for r in range(n_grp):
            gp = grp[r] if r == 0 else grp[r] + ends[r - 1]
            g_grp.append(gp)
            ends.append(bcast_row(gp, SUBLANES - 1))
        g = jnp.concatenate(g_grp, axis=0)
        inter = _dot_nt((q * jnp.exp(g)).astype(BF16), state_ref[hh].astype(BF16))
        diag = _dot_nt(q.astype(BF16), k.astype(BF16)) * mask_ref[0]
        return dict(q=q, k=k, g_grp=g_grp, ends=ends, inter=inter, scores=diag)

    def level(hd, m, lvl):
        g_grp, ends = hd["g_grp"], hd["ends"]
        if m >= SUBLANES:
            gm = m // SUBLANES
            refs = [ends[(r // (2 * gm)) * 2 * gm + gm - 1] for r in range(n_grp)]
            signs = [LOG2E if (SUBLANES * r) & m else -LOG2E for r in range(n_grp)]
        else:
            if m == 4:
                refs = [bcast_row(gp, 3) for gp in g_grp]
            elif m == 2:
                refs = [jnp.where(sub < 4, bcast_row(gp, 1), bcast_row(gp, 5)) for gp in g_grp]
            else:
                refs = [jnp.where((sub & 1) != 0, pltpu.roll(gp, 1, axis=0), gp) for gp in g_grp]
            signs = [sign_small[m]] * n_grp
        e = jnp.concatenate(
            [jnp.exp2((gp - rf) * sg) for gp, rf, sg in zip(g_grp, refs, signs)], axis=0)
        part = _dot_nt((hd["q"] * e).astype(BF16), (hd["k"] * e).astype(BF16))
        hd["scores"] = hd["scores"] + part * mask_ref[lvl]

    def finish(rows, hh, hd):
        cols = slice(hh * HG_DK, (hh + 1) * HG_DK)
        v = v_ref[rows, cols]
        o = hd["inter"] + _dot(hd["scores"].astype(BF16), v)
        g_end = hd["ends"][n_grp - 1]
        decay_to_end = jnp.exp(jnp.concatenate([g_end - gp for gp in hd["g_grp"]], axis=0))
        k_dec = (hd["k"] * decay_to_end).astype(BF16)
        state_ref[hh] = jnp.exp(g_end[0:1, :]) * state_ref[hh] + _dot_tn(v, k_dec)
        y = _rms_norm(o, gn_ref[:, cols]) * og_ref[rows, cols].astype(F32)
        o_ref[rows, cols] = y.astype(o_ref.dtype)

    def chunk(c, carry):
        rows = pl.ds(pl.multiple_of(c * HG_CHUNK, HG_CHUNK), HG_CHUNK)
        heads = [prepare(rows, hh) for hh in range(HEADS)]
        m, lvl = 1, 1
        while m < HG_CHUNK:
            for hd in heads:
                level(hd, m, lvl)
            m *= 2
            lvl += 1
        for hh, hd in enumerate(heads):
            finish(rows, hh, hd)
        return carry

    lax.fori_loop(0, n_chunks, chunk, 0)


def _hgrn_call(hq, lf, hk, hv, og, gnorm, *, batch, seq, tb):
    tokens = hq.shape[0]
    per_b = seq // tb
    width = HEADS * HG_DK
    blk = pl.BlockSpec((tb, width), lambda b, i: (b * per_b + i, 0))
    masks = jnp.asarray(_hgrn_masks())
    return pl.pallas_call(
        functools.partial(_hgrn_kernel, n_chunks=tb // HG_CHUNK),
        grid=(batch, per_b),
        in_specs=[blk, blk, blk, blk, blk, _resident((1, width)), _resident(masks.shape)],
        out_specs=blk,
        out_shape=jax.ShapeDtypeStruct((tokens, width), BF16),
        scratch_shapes=[pltpu.VMEM((HEADS, HG_DV, HG_DK), F32)],
        compiler_params=_params("arbitrary", "arbitrary"),
        name="hgrn",
    )(hq, lf, hk, hv, og, gnorm.reshape(1, width), masks)


def _attn_kernel(q_ref, k_ref, vt_ref, o_ref, m_ref, acc_ref, s_ref, *, tq, hps):
    qi = pl.program_id(2)
    m_ref[...] = jnp.full_like(m_ref, -jnp.inf)
    acc_ref[...] = jnp.zeros_like(acc_ref)
    ones = jnp.ones((ONES_ROWS, tq), BF16)

    def scores(a, j, slot):
        start = pl.multiple_of(j * tq, tq)
        s_ref[a, slot] = _dot_nt(k_ref[a, pl.ds(start, tq), :], q_ref[a])

    def accumulate(a, j, slot, causal):
        start = pl.multiple_of(j * tq, tq)
        s_t = s_ref[a, slot]
        if causal:
            kpos = lax.broadcasted_iota(jnp.int32, (tq, tq), 0)
            qpos = lax.broadcasted_iota(jnp.int32, (tq, tq), 1)
            s_t = jnp.where(kpos <= qpos, s_t, MASK_VALUE)
        m_old = m_ref[a]
        m_new = jnp.maximum(m_old, jnp.max(s_t, axis=0, keepdims=True))
        p = jnp.exp2((s_t - m_new).astype(BF16))
        alpha = jnp.exp2(m_old - m_new)
        v_ext = jnp.concatenate([vt_ref[a * MLA_V:(a + 1) * MLA_V, pl.ds(start, tq)], ones], axis=0)
        acc_ref[a] = alpha * acc_ref[a] + _dot(v_ext, p)
        m_ref[a] = m_new

    for a in range(hps):
        scores(a, 0, 0)

    def body(jj, carry):
        j = 2 * jj
        for a in range(hps):
            scores(a, j + 1, 1)
            accumulate(a, j, 0, False)
        for a in range(hps):
            scores(a, j + 2, 0)
            accumulate(a, j + 1, 1, False)
        return carry

    lax.fori_loop(0, qi // 2, body, 0)

    @pl.when(qi % 2 == 1)
    def _():
        for a in range(hps):
            scores(a, qi, 1)
            accumulate(a, qi - 1, 0, False)
        for a in range(hps):
            accumulate(a, qi, 1, True)

    @pl.when(qi % 2 == 0)
    def _():
        for a in range(hps):
            accumulate(a, qi, 0, True)

    for a in range(hps):
        acc = acc_ref[a]
        o_t = acc[0:MLA_V] * (1.0 / acc[MLA_V:MLA_V + 1])
        o_ref[:, a * MLA_V:(a + 1) * MLA_V] = o_t.T.astype(o_ref.dtype)


def _attn_call(q, k, vt, *, batch, seq, tq, hps):
    tokens = vt.shape[1]
    nq = seq // tq
    return pl.pallas_call(
        functools.partial(_attn_kernel, tq=tq, hps=hps),
        grid=(batch, HEADS // hps, nq),
        in_specs=[
            pl.BlockSpec((hps, tq, MLA_QK), lambda b, g, i: (g, b * nq + i, 0)),
            pl.BlockSpec((hps, seq, MLA_QK), lambda b, g, i: (g, b, 0)),
            pl.BlockSpec((hps * MLA_V, seq), lambda b, g, i: (g, b)),
        ],
        out_specs=pl.BlockSpec((tq, hps * MLA_V), lambda b, g, i: (b * nq + i, g)),
        out_shape=jax.ShapeDtypeStruct((tokens, HEADS * MLA_V), BF16),
        scratch_shapes=[pltpu.VMEM((hps, 1, tq), F32),
                        pltpu.VMEM((hps, MLA_V + ONES_ROWS, tq), F32),
                        pltpu.VMEM((hps, 2, tq, tq), F32)],
        compiler_params=_params("arbitrary", "arbitrary", "arbitrary"),
        name="mla_attn",
    )(q, k, vt)


def _merge_kernel(h_ref, ada_ref, yh_ref, om_ref, sga_ref, sgb_ref, wbh_ref, wbm_ref, wo_ref,
                  lng_ref, lnb_ref, o_ref):
    h = h_ref[...]
    gate = ada_ref[0, 5:6, :]
    y_hg = _dot(yh_ref[...], wbh_ref[...])
    y_mla = _dot(om_ref[...], wbm_ref[...])
    merged = sga_ref[...].astype(F32) * y_hg + sgb_ref[...].astype(F32) * y_mla
    y = _dot(merged.astype(BF16), wo_ref[...])
    r = DN_ALPHA * h + (1.0 + gate) * y
    o_ref[...] = _layer_norm(r, lng_ref[...], lnb_ref[...])


def _merge_call(h, ada_l, yh, om, sga, sgb, wbh, wbm, wo, lng, lnb, *, seq, tm):
    tokens = h.shape[0]
    per_b = seq // tm
    row = pl.BlockSpec((tm, D_MODEL), lambda i: (i, 0))
    sq = _resident((D_MODEL, D_MODEL))
    return pl.pallas_call(
        _merge_kernel,
        grid=(tokens // tm,),
        in_specs=[row, pl.BlockSpec((1, N_ADA, D_MODEL), lambda i: (i // per_b, 0, 0)),
                  row, row, row, row, sq, sq, sq,
                  _resident((1, D_MODEL)), _resident((1, D_MODEL))],
        out_specs=row,
        out_shape=jax.ShapeDtypeStruct((tokens, D_MODEL), F32),
        compiler_params=_params("arbitrary"),
        name="merge_out",
    )(h, ada_l, yh, om, sga, sgb, wbh, wbm, wo, lng, lnb)


def _mixer_weights(w_in, w_uq, w_ukv, q_norm_g, kv_norm_g):
    n = HEADS * HG_DK
    c_cq = 4 * n
    c_ckv = c_cq + Q_LORA
    c_kr = c_ckv + KV_LORA
    c_ga = c_kr + MLA_ROPE
    half = MLA_ROPE // 2
    kr = w_in[:, c_kr:c_ga]
    kr_swapped = jnp.concatenate([kr[:, half:], kr[:, :half]], axis=1)
    pad = jnp.zeros((D_MODEL, 128 - MLA_ROPE), w_in.dtype)
    w_lat = jnp.concatenate([w_in[:, c_cq:c_kr], kr, pad, kr_swapped, pad], axis=1)

    uq = w_uq.reshape(Q_LORA, HEADS, MLA_QK)
    uq_nope = uq[:, :, :MLA_NOPE].reshape(Q_LORA, HEADS * MLA_NOPE)
    uq_rope = uq[:, :, MLA_NOPE:]
    uq_rope_sw = jnp.concatenate([uq_rope[:, :, half:], uq_rope[:, :, :half]], axis=2)
    w_uq_p = jnp.concatenate([uq_nope, uq_rope.reshape(Q_LORA, HEADS * MLA_ROPE),
                              uq_rope_sw.reshape(Q_LORA, HEADS * MLA_ROPE)], axis=1)

    ukv = w_ukv.reshape(KV_LORA, HEADS, MLA_NOPE + MLA_V)
    w_uk = ukv[:, :, :MLA_NOPE].reshape(KV_LORA, HEADS * MLA_NOPE)
    w_vt = ukv[:, :, MLA_NOPE:].reshape(KV_LORA, HEADS * MLA_V).T
    return {
        "w_hg": w_in[:, :c_cq].astype(BF16),
        "w_lat": w_lat.astype(BF16),
        "w_gate": w_in[:, c_ga:].astype(BF16),
        "w_uq": w_uq_p.astype(BF16),
        "w_uk": w_uk.astype(BF16),
        "w_vt": w_vt.astype(BF16),
        "q_norm_g": q_norm_g.reshape(1, Q_LORA),
        "kv_norm_g": kv_norm_g.reshape(1, KV_LORA),
    }


def _tile(seq, want):
    t = min(want, seq)
    assert seq % t == 0, (seq, t)
    return t


def kernel(x, c, positions, ada_w, ada_b, ln_g, ln_b, ffn1_gate, ffn1_up, ffn1_down, w_in,
           hg_lower_bound, hg_norm_g, mla_q_norm_g, mla_w_uq, mla_kv_norm_g, mla_w_ukv,
           w_branch_hg, w_branch_mla, w_out, ffn2_gate, ffn2_up, ffn2_down):
    batch, seq, _ = x.shape
    tokens = batch * seq
    tm_ffn = _tile(seq, 512)
    tm_in = _tile(seq, 256)
    tm_merge = _tile(seq, 512)
    tb_hgrn = _tile(seq, 512)
    tq = _tile(seq, 512)
    attn_heads_per_step = 2

    ada = _ada_call(c, ada_w, ada_b).reshape(DEPTH, batch, N_ADA, D_MODEL)
    cos_t, sin_t = _rope_call(positions)
    lbraw = hg_lower_bound.astype(F32)

    h = x.reshape(tokens, D_MODEL)
    for l in range(DEPTH):
        ada_l = ada[l]
        ln = lambda i: (ln_g[l, i].reshape(1, D_MODEL), ln_b[l, i].reshape(1, D_MODEL))

        h = _ffn_call(h, ada_l, ffn1_gate[l].astype(BF16), ffn1_up[l].astype(BF16),
                      ffn1_down[l].astype(BF16), *ln(0), k0=0, seq=seq, tm=tm_ffn)

        w = _mixer_weights(w_in[l], mla_w_uq[l], mla_w_ukv[l], mla_q_norm_g[l], mla_kv_norm_g[l])
        hq, lf, hk, hv, og, q, k, vt, sga, sgb = _inproj_call(
            h, ada_l, w, lbraw, cos_t, sin_t, layer=l, seq=seq, tm=tm_in)
        yh = _hgrn_call(hq, lf, hk, hv, og, hg_norm_g[l], batch=batch, seq=seq, tb=tb_hgrn)
        om = _attn_call(q, k, vt, batch=batch, seq=seq, tq=tq, hps=attn_heads_per_step)
        h = _merge_call(h, ada_l, yh, om, sga, sgb, w_branch_hg[l].astype(BF16),
                        w_branch_mla[l].astype(BF16), w_out[l].astype(BF16), *ln(1),
                        seq=seq, tm=tm_merge)

        h = _ffn_call(h, ada_l, ffn2_gate[l].astype(BF16), ffn2_up[l].astype(BF16),
                      ffn2_down[l].astype(BF16), *ln(2), k0=6, seq=seq, tm=tm_ffn)
    return h.reshape(batch, seq, D_MODEL)
```

```python
import functools

import jax
import jax.numpy as jnp
import numpy as np
from jax import lax
from jax.experimental import pallas as pl
from jax.experimental.pallas import tpu as pltpu

F32 = jnp.float32
BF16 = jnp.bfloat16

D_MODEL = 1024
DEPTH = 4
HEADS = 8
HG_DK = 128
HG_DV = 128
HG_CHUNK = 64
LB_FLOOR = 1e-30
MLA_NOPE = 128
MLA_ROPE = 64
MLA_QK = MLA_NOPE + MLA_ROPE
MLA_V = 128
Q_LORA = 384
KV_LORA = 256
ROPE_THETA = 10000.0
MASK_VALUE = -1e30
D_FF = 2816
DN_ALPHA = (2.0 * DEPTH) ** 0.25
LN_EPS = 1e-5
RMS_EPS = 1e-6
N_ADA = 9
LOG2E = 1.4426950408889634
ATTN_SCALE_LOG2E = float(MLA_QK) ** -0.5 * LOG2E
SUBLANES = 8
ONES_ROWS = 16

VMEM_LIMIT_BYTES = 56 * 1024 * 1024


def _params(*sem):
    return pltpu.CompilerParams(dimension_semantics=sem, vmem_limit_bytes=VMEM_LIMIT_BYTES)


def _resident(shape):
    zeros = (0,) * len(shape)
    return pl.BlockSpec(shape, lambda *_: zeros, pipeline_mode=pl.Buffered(1))


def _silu(x):
    return x * jax.nn.sigmoid(x)


def _dot(a, b):
    return jnp.dot(a, b, preferred_element_type=F32)


def _dot_nt(a, b):
    return lax.dot_general(a, b, (((1,), (1,)), ((), ())), preferred_element_type=F32)


def _dot_tn(a, b):
    return lax.dot_general(a, b, (((0,), (0,)), ((), ())), preferred_element_type=F32)


def _layer_norm(r, g, b):
    mu = jnp.mean(r, axis=-1, keepdims=True)
    d = r - mu
    var = jnp.mean(d * d, axis=-1, keepdims=True)
    return d * lax.rsqrt(var + LN_EPS) * g + b


def _rms_norm(x, g):
    return x * lax.rsqrt(jnp.mean(x * x, axis=-1, keepdims=True) + RMS_EPS) * g


def _ada_kernel(c_ref, w_ref, b_ref, o_ref):
    cond = _silu(c_ref[...])
    o_ref[0] = jnp.dot(cond, w_ref[0], preferred_element_type=F32,
                       precision=lax.Precision.HIGHEST) + b_ref[0]


def _ada_call(c, ada_w, ada_b):
    batch = c.shape[0]
    n_out = ada_w.shape[-1]
    tn = D_MODEL
    return pl.pallas_call(
        _ada_kernel,
        grid=(DEPTH, n_out // tn),
        in_specs=[
            pl.BlockSpec((batch, D_MODEL), lambda l, j: (0, 0)),
            pl.BlockSpec((1, D_MODEL, tn), lambda l, j: (l, 0, j)),
            pl.BlockSpec((1, 1, tn), lambda l, j: (l, 0, j)),
        ],
        out_specs=pl.BlockSpec((1, batch, tn), lambda l, j: (l, 0, j)),
        out_shape=jax.ShapeDtypeStruct((DEPTH, batch, n_out), F32),
        compiler_params=_params("arbitrary", "arbitrary"),
        name="ada",
    )(c, ada_w, ada_b.reshape(DEPTH, 1, n_out))


def _rope_kernel(pos_ref, inv_ref, sign_ref, cos_ref, sin_ref):
    ang = pos_ref[...].astype(F32) * inv_ref[...]
    cos_ref[...] = jnp.cos(ang)
    sin_ref[...] = jnp.sin(ang) * sign_ref[...]


def _rope_call(positions):
    tokens = positions.size
    tm = min(1024, tokens)
    half = MLA_ROPE // 2
    inv = 1.0 / (ROPE_THETA ** (jnp.arange(0, MLA_ROPE, 2, dtype=F32) / MLA_ROPE))
    inv_t = jnp.tile(inv, 4).reshape(1, 4 * half)
    sign = jnp.tile(jnp.concatenate([-jnp.ones((half,), F32), jnp.ones((half,), F32)]), 2)
    sign = sign.reshape(1, 4 * half)
    row = pl.BlockSpec((tm, 4 * half), lambda i: (i, 0))
    const = pl.BlockSpec((1, 4 * half), lambda i: (0, 0))
    return pl.pallas_call(
        _rope_kernel,
        grid=(tokens // tm,),
        in_specs=[pl.BlockSpec((tm, 1), lambda i: (i, 0)), const, const],
        out_specs=[row, row],
        out_shape=[jax.ShapeDtypeStruct((tokens, 4 * half), F32)] * 2,
        compiler_params=_params("arbitrary"),
        name="rope_tables",
    )(positions.reshape(tokens, 1), inv_t, sign)


def _ffn_kernel(h_ref, ada_ref, wg_ref, wu_ref, wd_ref, lng_ref, lnb_ref, o_ref, *, k0):
    h = h_ref[...]
    shift = ada_ref[0, k0:k0 + 1, :]
    scale = ada_ref[0, k0 + 1:k0 + 2, :]
    gate = ada_ref[0, k0 + 2:k0 + 3, :]
    u = (h * (1.0 + scale) + shift).astype(BF16)
    g = _dot(u, wg_ref[...])
    up = _dot(u, wu_ref[...])
    a = (_silu(g) * up).astype(BF16)
    y = _dot(a, wd_ref[...])
    r = DN_ALPHA * h + (0.5 * (1.0 + gate)) * y
    o_ref[...] = _layer_norm(r, lng_ref[...], lnb_ref[...])


def _ffn_call(h, ada_l, wg, wu, wd, lng, lnb, *, k0, seq, tm):
    tokens = h.shape[0]
    per_b = seq // tm
    row = pl.BlockSpec((tm, D_MODEL), lambda i: (i, 0))
    return pl.pallas_call(
        functools.partial(_ffn_kernel, k0=k0),
        grid=(tokens // tm,),
        in_specs=[
            row,
            pl.BlockSpec((1, N_ADA, D_MODEL), lambda i: (i // per_b, 0, 0)),
            _resident((D_MODEL, D_FF)),
            _resident((D_MODEL, D_FF)),
            _resident((D_FF, D_MODEL)),
            _resident((1, D_MODEL)),
            _resident((1, D_MODEL)),
        ],
        out_specs=row,
        out_shape=jax.ShapeDtypeStruct((tokens, D_MODEL), F32),
        compiler_params=_params("arbitrary"),
        name="ffn",
    )(h, ada_l, wg, wu, wd, lng, lnb)


def _inproj_kernel(h_ref, ada_ref, whg_ref, wlat_ref, wgate_ref, lbraw_ref, qng_ref, wuq_ref,
                   kvng_ref, wuk_ref, wvt_ref, cos_ref, sin_ref,
                   hq_ref, lf_ref, hk_ref, hv_ref, og_ref, q_ref, k_ref, vt_ref, sga_ref, sgb_ref,
                   *, layer):
    h = h_ref[...]
    shift = ada_ref[0, 3:4, :]
    scale = ada_ref[0, 4:5, :]
    u = (h * (1.0 + scale) + shift).astype(BF16)

    raw = lbraw_ref[...]
    e = jnp.exp(raw - jnp.max(raw, axis=0, keepdims=True))
    lb = jnp.zeros((1, HEADS * HG_DK), F32)
    for i in range(1, layer + 1):
        lb = lb + e[i:i + 1, :]
    lb = lb / jnp.sum(e, axis=0, keepdims=True)
    one_minus_lb = 1.0 - lb

    n = HEADS * HG_DK
    zf = _dot(u, whg_ref[:, n:2 * n])
    t = jnp.exp(-jnp.abs(zf))
    r = 1.0 / (1.0 + t)
    tr = t * r
    pos = zf >= 0.0
    sig_pos = jnp.where(pos, r, tr)
    sig_neg = jnp.where(pos, tr, r)
    lf_ref[...] = jnp.log2(jnp.maximum(lb, LB_FLOOR) + one_minus_lb * sig_pos)
    hk_ref[...] = (one_minus_lb * sig_neg).astype(BF16)

    zgate = _dot(u, wgate_ref[...])
    sga_ref[...] = jax.nn.sigmoid(zgate[:, 0:D_MODEL]).astype(sga_ref.dtype)
    sgb_ref[...] = jax.nn.sigmoid(zgate[:, D_MODEL:2 * D_MODEL]).astype(sgb_ref.dtype)

    zl = _dot(u, wlat_ref[...])
    cos_t = cos_ref[...]
    sin_t = sin_ref[...]
    cqn = _rms_norm(zl[:, 0:Q_LORA], qng_ref[...]).astype(BF16)
    qf = _dot(cqn, wuq_ref[...]) * ATTN_SCALE_LOG2E
    nq = HEADS * MLA_NOPE
    nr = HEADS * MLA_ROPE
    cos_q = jnp.concatenate([cos_t] * (nr // 128), axis=1)
    sin_q = jnp.concatenate([sin_t] * (nr // 128), axis=1)
    q_rope = qf[:, nq:nq + nr] * cos_q + qf[:, nq + nr:nq + 2 * nr] * sin_q
    ckvn = _rms_norm(zl[:, Q_LORA:Q_LORA + KV_LORA], kvng_ref[...]).astype(BF16)
    k_nope = _dot(ckvn, wuk_ref[...])
    vt_ref[...] = _dot_nt(wvt_ref[...], ckvn).astype(BF16)
    c0 = Q_LORA + KV_LORA
    k_rope = (zl[:, c0:c0 + 128] * cos_t + zl[:, c0 + 128:c0 + 256] * sin_t)[:, 0:MLA_ROPE]
    k_rope = k_rope.astype(BF16)
    for hh in range(HEADS):
        q_ref[hh, :, 0:MLA_NOPE] = qf[:, hh * MLA_NOPE:(hh + 1) * MLA_NOPE].astype(BF16)
        q_ref[hh, :, MLA_NOPE:MLA_QK] = q_rope[:, hh * MLA_ROPE:(hh + 1) * MLA_ROPE].astype(BF16)
        k_ref[hh, :, 0:MLA_NOPE] = k_nope[:, hh * MLA_NOPE:(hh + 1) * MLA_NOPE].astype(BF16)
        k_ref[hh, :, MLA_NOPE:MLA_QK] = k_rope

    hq_ref[...] = _silu(_dot(u, whg_ref[:, 0:n])).astype(BF16)
    og_ref[...] = _silu(_dot(u, whg_ref[:, 3 * n:4 * n])).astype(og_ref.dtype)
    hv_ref[...] = _dot(u, whg_ref[:, 2 * n:3 * n]).astype(BF16)


def _inproj_call(h, ada_l, w, lbraw, cos_t, sin_t, *, layer, seq, tm):
    tokens = h.shape[0]
    per_b = seq // tm
    row = pl.BlockSpec((tm, D_MODEL), lambda i: (i, 0))
    tab = pl.BlockSpec((tm, 128), lambda i: (i, 0))
    headed = pl.BlockSpec((HEADS, tm, MLA_QK), lambda i: (0, i, 0))
    wide = jax.ShapeDtypeStruct((tokens, D_MODEL), BF16)
    qk = jax.ShapeDtypeStruct((HEADS, tokens, MLA_QK), BF16)
    return pl.pallas_call(
        functools.partial(_inproj_kernel, layer=layer),
        grid=(tokens // tm,),
        in_specs=[
            row,
            pl.BlockSpec((1, N_ADA, D_MODEL), lambda i: (i // per_b, 0, 0)),
            _resident(w["w_hg"].shape),
            _resident(w["w_lat"].shape),
            _resident(w["w_gate"].shape),
            _resident(lbraw.shape),
            _resident((1, Q_LORA)),
            _resident(w["w_uq"].shape),
            _resident((1, KV_LORA)),
            _resident(w["w_uk"].shape),
            _resident(w["w_vt"].shape),
            tab, tab,
        ],
        out_specs=[row, row, row, row, row, headed, headed,
                   pl.BlockSpec((HEADS * MLA_V, tm), lambda i: (0, i)), row, row],
        out_shape=[wide, jax.ShapeDtypeStruct((tokens, D_MODEL), F32), wide, wide, wide,
                   qk, qk, jax.ShapeDtypeStruct((HEADS * MLA_V, tokens), BF16), wide, wide],
        compiler_params=_params("arbitrary"),
        name="inproj",
    )(h, ada_l, w["w_hg"], w["w_lat"], w["w_gate"], lbraw, w["q_norm_g"], w["w_uq"],
      w["kv_norm_g"], w["w_uk"], w["w_vt"], cos_t, sin_t)


def _hgrn_masks():
    t = np.arange(HG_CHUNK)[:, None]
    s = np.arange(HG_CHUNK)[None, :]
    masks = [(t == s)]
    m = 1
    while m < HG_CHUNK:
        masks.append((t // (2 * m) == s // (2 * m)) & ((t & m) != 0) & ((s & m) == 0))
        m *= 2
    return np.stack(masks).astype(np.float32)


def _hgrn_kernel(q_ref, lf_ref, k_ref, v_ref, og_ref, gn_ref, mask_ref, o_ref, state_ref, *, n_chunks):
    @pl.when(pl.program_id(1) == 0)
    def _():
        state_ref[...] = jnp.zeros_like(state_ref)

    n_grp = HG_CHUNK // SUBLANES
    sub = lax.broadcasted_iota(jnp.int32, (SUBLANES, HG_DK), 0)
    sign_small = {m: jnp.where((sub & m) != 0, 1.0, -1.0) for m in (1, 2, 4)}

    def bcast_row(x, i):
        return jnp.broadcast_to(x[i:i + 1, :], (SUBLANES, HG_DK))

    def prepare(rows, hh):
        cols = slice(hh * HG_DK, (hh + 1) * HG_DK)
        q_bf = q_ref[rows, cols]
        k_bf = k_ref[rows, cols]
        q = q_bf.astype(F32)
        k = k_bf.astype(F32)

        lf = lf_ref[rows, cols]
        grp = [lf[SUBLANES * r:SUBLANES * (r + 1)] for r in range(n_grp)]
        for sh in (1, 2, 4):
            grp = [p + jnp.where(sub >= sh, pltpu.roll(p, sh, axis=0), 0.0) for p in grp]
        g_grp, ends = [], []
        for r in range(n_grp):
            gp = grp[r] if r == 0 else grp[r] + ends[r - 1]
            g_grp.append(gp)
            ends.append(bcast_row(gp, SUBLANES - 1))
        from_start = jnp.exp2(jnp.concatenate(g_grp, axis=0))
        inter = _dot_nt((q * from_start).astype(BF16), state_ref[hh].astype(BF16))
        diag = _dot_nt(q_bf, k_bf) * mask_ref[0]
        return dict(q=q, k=k, g_grp=g_grp, ends=ends, inter=inter, scores=diag)

    def level(hd, m, lvl):
        g_grp, ends = hd["g_grp"], hd["ends"]
        if m >= SUBLANES:
            gm = m // SUBLANES
            refs = [ends[(r // (2 * gm)) * 2 * gm + gm - 1] for r in range(n_grp)]
            args = [gp - rf if (SUBLANES * r) & m else rf - gp
                    for r, (gp, rf) in enumerate(zip(g_grp, refs))]
        else:
            if m == 4:
                refs = [bcast_row(gp, 3) for gp in g_grp]
            elif m == 2:
                refs = [jnp.where(sub < 4, bcast_row(gp, 1), bcast_row(gp, 5)) for gp in g_grp]
            else:
                refs = [jnp.where((sub & 1) != 0, pltpu.roll(gp, 1, axis=0), gp) for gp in g_grp]
            args = [(gp - rf) * sign_small[m] for gp, rf in zip(g_grp, refs)]
        e = jnp.exp2(jnp.concatenate(args, axis=0))
        part = _dot_nt((hd["q"] * e).astype(BF16), (hd["k"] * e).astype(BF16))
        hd["scores"] = hd["scores"] + part * mask_ref[lvl]

    def finish(rows, hh, hd):
        cols = slice(hh * HG_DK, (hh + 1) * HG_DK)
        v = v_ref[rows, cols]
        o = hd["inter"] + _dot(hd["scores"].astype(BF16), v)
        g_end = hd["ends"][n_grp - 1]
        to_end = jnp.exp2(jnp.concatenate([g_end - gp for gp in hd["g_grp"]], axis=0))
        k_dec = (hd["k"] * to_end).astype(BF16)
        state_ref[hh] = jnp.exp2(g_end[0:1, :]) * state_ref[hh] + _dot_tn(v, k_dec)
        y = _rms_norm(o, gn_ref[:, cols]) * og_ref[rows, cols].astype(F32)
        o_ref[rows, cols] = y.astype(o_ref.dtype)

    def chunk(c, carry):
        rows = pl.ds(pl.multiple_of(c * HG_CHUNK, HG_CHUNK), HG_CHUNK)
        heads = [prepare(rows, hh) for hh in range(HEADS)]
        m, lvl = 1, 1
        while m < HG_CHUNK:
            for hd in heads:
                level(hd, m, lvl)
            m *= 2
            lvl += 1
        for hh, hd in enumerate(heads):
            finish(rows, hh, hd)
        return carry

    lax.fori_loop(0, n_chunks, chunk, 0)


def _hgrn_call(hq, lf, hk, hv, og, gnorm, *, batch, seq, tb):
    tokens = hq.shape[0]
    per_b = seq // tb
    width = HEADS * HG_DK
    blk = pl.BlockSpec((tb, width), lambda b, i: (b * per_b + i, 0))
    masks = jnp.asarray(_hgrn_masks())
    return pl.pallas_call(
        functools.partial(_hgrn_kernel, n_chunks=tb // HG_CHUNK),
        grid=(batch, per_b),
        in_specs=[blk, blk, blk, blk, blk, _resident((1, width)), _resident(masks.shape)],
        out_specs=blk,
        out_shape=jax.ShapeDtypeStruct((tokens, width), BF16),
        scratch_shapes=[pltpu.VMEM((HEADS, HG_DV, HG_DK), F32)],
        compiler_params=_params("arbitrary", "arbitrary"),
        name="hgrn",
    )(hq, lf, hk, hv, og, gnorm.reshape(1, width), masks)


def _attn_kernel(q_ref, k_ref, vt_ref, o_ref, m_ref, acc_ref, s_ref, bmax_ref, *, tq, hps):
    qi = pl.program_id(2)
    m_ref[...] = jnp.full_like(m_ref, -jnp.inf)
    acc_ref[...] = jnp.zeros_like(acc_ref)
    ones = jnp.ones((ONES_ROWS, tq), BF16)

    def scores(a, j, slot):
        start = pl.multiple_of(j * tq, tq)
        s_t = _dot_nt(k_ref[a, pl.ds(start, tq), :], q_ref[a])
        s_ref[a, slot] = s_t
        bmax_ref[a, slot] = jnp.max(s_t, axis=0, keepdims=True)

    def accumulate(a, j, slot, causal):
        start = pl.multiple_of(j * tq, tq)
        s_t = s_ref[a, slot]
        if causal:
            kpos = lax.broadcasted_iota(jnp.int32, (tq, tq), 0)
            qpos = lax.broadcasted_iota(jnp.int32, (tq, tq), 1)
            s_t = jnp.where(kpos <= qpos, s_t, MASK_VALUE)
            block_max = jnp.max(s_t, axis=0, keepdims=True)
        else:
            block_max = bmax_ref[a, slot]
        m_old = m_ref[a]
        m_new = jnp.maximum(m_old, block_max)
        p = jnp.exp2((s_t - m_new).astype(BF16))
        alpha = jnp.exp2(m_old - m_new)
        v_ext = jnp.concatenate([vt_ref[a * MLA_V:(a + 1) * MLA_V, pl.ds(start, tq)], ones], axis=0)
        acc_ref[a] = alpha * acc_ref[a] + _dot(v_ext, p)
        m_ref[a] = m_new

    for a in range(hps):
        scores(a, 0, 0)

    def step(j, slot):
        for a in range(hps):
            scores(a, j + 1, 1 - slot)
            accumulate(a, j, slot, False)

    def body(jj, carry):
        step(2 * jj, 0)
        step(2 * jj + 1, 1)
        return carry

    lax.fori_loop(0, qi // 2, body, 0)

    @pl.when(qi % 2 == 1)
    def _():
        step(qi - 1, 0)
        for a in range(hps):
            accumulate(a, qi, 1, True)

    @pl.when(qi % 2 == 0)
    def _():
        for a in range(hps):
            accumulate(a, qi, 0, True)

    for a in range(hps):
        acc = acc_ref[a]
        o_t = acc[0:MLA_V] * (1.0 / acc[MLA_V:MLA_V + 1])
        o_ref[:, a * MLA_V:(a + 1) * MLA_V] = o_t.T.astype(o_ref.dtype)


def _attn_call(q, k, vt, *, batch, seq, tq, hps):
    tokens = vt.shape[1]
    nq = seq // tq
    return pl.pallas_call(
        functools.partial(_attn_kernel, tq=tq, hps=hps),
        grid=(batch, HEADS // hps, nq),
        in_specs=[
            pl.BlockSpec((hps, tq, MLA_QK), lambda b, g, i: (g, b * nq + i, 0)),
            pl.BlockSpec((hps, seq, MLA_QK), lambda b, g, i: (g, b, 0)),
            pl.BlockSpec((hps * MLA_V, seq), lambda b, g, i: (g, b)),
        ],
        out_specs=pl.BlockSpec((tq, hps * MLA_V), lambda b, g, i: (b * nq + i, g)),
        out_shape=jax.ShapeDtypeStruct((tokens, HEADS * MLA_V), BF16),
        scratch_shapes=[pltpu.VMEM((hps, 1, tq), F32),
                        pltpu.VMEM((hps, MLA_V + ONES_ROWS, tq), F32),
                        pltpu.VMEM((hps, 2, tq, tq), F32),
                        pltpu.VMEM((hps, 2, 1, tq), F32)],
        compiler_params=_params("arbitrary", "arbitrary", "arbitrary"),
        name="mla_attn",
    )(q, k, vt)


def _merge_kernel(h_ref, ada_ref, yh_ref, om_ref, sga_ref, sgb_ref, wbh_ref, wbm_ref, wo_ref,
                  lng_ref, lnb_ref, o_ref):
    h = h_ref[...]
    gate = ada_ref[0, 5:6, :]
    y_hg = _dot(yh_ref[...], wbh_ref[...])
    y_mla = _dot(om_ref[...], wbm_ref[...])
    merged = sga_ref[...].astype(F32) * y_hg + sgb_ref[...].astype(F32) * y_mla
    y = _dot(merged.astype(BF16), wo_ref[...])
    r = DN_ALPHA * h + (1.0 + gate) * y
    o_ref[...] = _layer_norm(r, lng_ref[...], lnb_ref[...])


def _merge_call(h, ada_l, yh, om, sga, sgb, wbh, wbm, wo, lng, lnb, *, seq, tm):
    tokens = h.shape[0]
    per_b = seq // tm
    row = pl.BlockSpec((tm, D_MODEL), lambda i: (i, 0))
    sq = _resident((D_MODEL, D_MODEL))
    return pl.pallas_call(
        _merge_kernel,
        grid=(tokens // tm,),
        in_specs=[row, pl.BlockSpec((1, N_ADA, D_MODEL), lambda i: (i // per_b, 0, 0)),
                  row, row, row, row, sq, sq, sq,
                  _resident((1, D_MODEL)), _resident((1, D_MODEL))],
        out_specs=row,
        out_shape=jax.ShapeDtypeStruct((tokens, D_MODEL), F32),
        compiler_params=_params("arbitrary"),
        name="merge_out",
    )(h, ada_l, yh, om, sga, sgb, wbh, wbm, wo, lng, lnb)


def _mixer_weights(w_in, w_uq, w_ukv, q_norm_g, kv_norm_g):
    n = HEADS * HG_DK
    c_cq = 4 * n
    c_ckv = c_cq + Q_LORA
    c_kr = c_ckv + KV_LORA
    c_ga = c_kr + MLA_ROPE
    half = MLA_ROPE // 2
    kr = w_in[:, c_kr:c_ga]
    kr_swapped = jnp.concatenate([kr[:, half:], kr[:, :half]], axis=1)
    pad = jnp.zeros((D_MODEL, 128 - MLA_ROPE), w_in.dtype)
    w_lat = jnp.concatenate([w_in[:, c_cq:c_kr], kr, pad, kr_swapped, pad], axis=1)

    uq = w_uq.reshape(Q_LORA, HEADS, MLA_QK)
    uq_nope = uq[:, :, :MLA_NOPE].reshape(Q_LORA, HEADS * MLA_NOPE)
    uq_rope = uq[:, :, MLA_NOPE:]
    uq_rope_sw = jnp.concatenate([uq_rope[:, :, half:], uq_rope[:, :, :half]], axis=2)
    w_uq_p = jnp.concatenate([uq_nope, uq_rope.reshape(Q_LORA, HEADS * MLA_ROPE),
                              uq_rope_sw.reshape(Q_LORA, HEADS * MLA_ROPE)], axis=1)

    ukv = w_ukv.reshape(KV_LORA, HEADS, MLA_NOPE + MLA_V)
    w_uk = ukv[:, :, :MLA_NOPE].reshape(KV_LORA, HEADS * MLA_NOPE)
    w_vt = ukv[:, :, MLA_NOPE:].reshape(KV_LORA, HEADS * MLA_V).T
    return {
        "w_hg": w_in[:, :c_cq].astype(BF16),
        "w_lat": w_lat.astype(BF16),
        "w_gate": w_in[:, c_ga:].astype(BF16),
        "w_uq": w_uq_p.astype(BF16),
        "w_uk": w_uk.astype(BF16),
        "w_vt": w_vt.astype(BF16),
        "q_norm_g": q_norm_g.reshape(1, Q_LORA),
        "kv_norm_g": kv_norm_g.reshape(1, KV_LORA),
    }


def _tile(seq, want):
    t = min(want, seq)
    assert seq % t == 0, (seq, t)
    return t


def kernel(x, c, positions, ada_w, ada_b, ln_g, ln_b, ffn1_gate, ffn1_up, ffn1_down, w_in,
           hg_lower_bound, hg_norm_g, mla_q_norm_g, mla_w_uq, mla_kv_norm_g, mla_w_ukv,
           w_branch_hg, w_branch_mla, w_out, ffn2_gate, ffn2_up, ffn2_down):
    batch, seq, _ = x.shape
    tokens = batch * seq
    tm_ffn = _tile(seq, 512)
    tm_in = _tile(seq, 256)
    tm_merge = _tile(seq, 512)
    tb_hgrn = _tile(seq, 512)
    tq = _tile(seq, 512)
    attn_heads_per_step = 2

    ada = _ada_call(c, ada_w, ada_b).reshape(DEPTH, batch, N_ADA, D_MODEL)
    cos_t, sin_t = _rope_call(positions)
    lbraw = hg_lower_bound.astype(F32)

    h = x.reshape(tokens, D_MODEL)
    for l in range(DEPTH):
        ada_l = ada[l]
        ln = lambda i: (ln_g[l, i].reshape(1, D_MODEL), ln_b[l, i].reshape(1, D_MODEL))

        h = _ffn_call(h, ada_l, ffn1_gate[l].astype(BF16), ffn1_up[l].astype(BF16),
                      ffn1_down[l].astype(BF16), *ln(0), k0=0, seq=seq, tm=tm_ffn)

        w = _mixer_weights(w_in[l], mla_w_uq[l], mla_w_ukv[l], mla_q_norm_g[l], mla_kv_norm_g[l])
        hq, lf, hk, hv, og, q, k, vt, sga, sgb = _inproj_call(
            h, ada_l, w, lbraw, cos_t, sin_t, layer=l, seq=seq, tm=tm_in)
        yh = _hgrn_call(hq, lf, hk, hv, og, hg_norm_g[l], batch=batch, seq=seq, tb=tb_hgrn)
        om = _attn_call(q, k, vt, batch=batch, seq=seq, tq=tq, hps=attn_heads_per_step)
        h = _merge_call(h, ada_l, yh, om, sga, sgb, w_branch_hg[l].astype(BF16),
                        w_branch_mla[l].astype(BF16), w_out[l].astype(BF16), *ln(1),
                        seq=seq, tm=tm_merge)

        h = _ffn_call(h, ada_l, ffn2_gate[l].astype(BF16), ffn2_up[l].astype(BF16),
                      ffn2_down[l].astype(BF16), *ln(2), k0=6, seq=seq, tm=tm_ffn)
    return h.reshape(batch, seq, D_MODEL)
```

```python
import functools

import jax
import jax.numpy as jnp
import numpy as np
from jax import lax
from jax.experimental import pallas as pl
from jax.experimental.pallas import tpu as pltpu

F32 = jnp.float32
BF16 = jnp.bfloat16

D_MODEL = 1024
DEPTH = 4
HEADS = 8
HG_DK = 128
HG_DV = 128
HG_CHUNK = 64
LB_FLOOR = 1e-30
MLA_NOPE = 128
MLA_ROPE = 64
MLA_QK = MLA_NOPE + MLA_ROPE
MLA_V = 128
Q_LORA = 384
KV_LORA = 256
ROPE_THETA = 10000.0
MASK_VALUE = -1e30
D_FF = 2816
DN_ALPHA = (2.0 * DEPTH) ** 0.25
LN_EPS = 1e-5
RMS_EPS = 1e-6
N_ADA = 9
LOG2E = 1.4426950408889634
ATTN_SCALE_LOG2E = float(MLA_QK) ** -0.5 * LOG2E
SUBLANES = 8
ONES_ROWS = 16

VMEM_LIMIT_BYTES = 56 * 1024 * 1024


def _params(*sem):
    return pltpu.CompilerParams(dimension_semantics=sem, vmem_limit_bytes=VMEM_LIMIT_BYTES)


def _resident(shape):
    zeros = (0,) * len(shape)
    return pl.BlockSpec(shape, lambda *_: zeros, pipeline_mode=pl.Buffered(1))


def _silu(x):
    return x * jax.nn.sigmoid(x)


def _dot(a, b):
    return jnp.dot(a, b, preferred_element_type=F32)


def _dot_nt(a, b):
    return lax.dot_general(a, b, (((1,), (1,)), ((), ())), preferred_element_type=F32)


def _dot_tn(a, b):
    return lax.dot_general(a, b, (((0,), (0,)), ((), ())), preferred_element_type=F32)


def _row_halves(rows):
    return [slice(0, rows // 2), slice(rows // 2, rows)]


def _layer_norm(r, g, b):
    mu = jnp.mean(r, axis=-1, keepdims=True)
    d = r - mu
    var = jnp.mean(d * d, axis=-1, keepdims=True)
    return d * lax.rsqrt(var + LN_EPS) * g + b


def _rms_norm(x, g):
    return x * lax.rsqrt(jnp.mean(x * x, axis=-1, keepdims=True) + RMS_EPS) * g


def _ada_kernel(c_ref, w_ref, b_ref, o_ref):
    cond = _silu(c_ref[...])
    o_ref[0] = jnp.dot(cond, w_ref[0], preferred_element_type=F32,
                       precision=lax.Precision.HIGHEST) + b_ref[0]


def _ada_call(c, ada_w, ada_b):
    batch = c.shape[0]
    n_out = ada_w.shape[-1]
    tn = D_MODEL
    return pl.pallas_call(
        _ada_kernel,
        grid=(DEPTH, n_out // tn),
        in_specs=[
            pl.BlockSpec((batch, D_MODEL), lambda l, j: (0, 0)),
            pl.BlockSpec((1, D_MODEL, tn), lambda l, j: (l, 0, j)),
            pl.BlockSpec((1, 1, tn), lambda l, j: (l, 0, j)),
        ],
        out_specs=pl.BlockSpec((1, batch, tn), lambda l, j: (l, 0, j)),
        out_shape=jax.ShapeDtypeStruct((DEPTH, batch, n_out), F32),
        compiler_params=_params("arbitrary", "arbitrary"),
        name="ada",
    )(c, ada_w, ada_b.reshape(DEPTH, 1, n_out))


def _rope_kernel(pos_ref, inv_ref, sign_ref, cos_ref, sin_ref):
    ang = pos_ref[...].astype(F32) * inv_ref[...]
    cos_ref[...] = jnp.cos(ang)
    sin_ref[...] = jnp.sin(ang) * sign_ref[...]


def _rope_call(positions):
    tokens = positions.size
    tm = min(1024, tokens)
    half = MLA_ROPE // 2
    inv = 1.0 / (ROPE_THETA ** (jnp.arange(0, MLA_ROPE, 2, dtype=F32) / MLA_ROPE))
    inv_t = jnp.tile(inv, 4).reshape(1, 4 * half)
    sign = jnp.tile(jnp.concatenate([-jnp.ones((half,), F32), jnp.ones((half,), F32)]), 2)
    sign = sign.reshape(1, 4 * half)
    row = pl.BlockSpec((tm, 4 * half), lambda i: (i, 0))
    const = pl.BlockSpec((1, 4 * half), lambda i: (0, 0))
    return pl.pallas_call(
        _rope_kernel,
        grid=(tokens // tm,),
        in_specs=[pl.BlockSpec((tm, 1), lambda i: (i, 0)), const, const],
        out_specs=[row, row],
        out_shape=[jax.ShapeDtypeStruct((tokens, 4 * half), F32)] * 2,
        compiler_params=_params("arbitrary"),
        name="rope_tables",
    )(positions.reshape(tokens, 1), inv_t, sign)


def _ffn_kernel(h_ref, ada_ref, wg_ref, wu_ref, wd_ref, lng_ref, lnb_ref, o_ref, *, k0):
    shift = ada_ref[0, k0:k0 + 1, :]
    scale = ada_ref[0, k0 + 1:k0 + 2, :]
    gate = ada_ref[0, k0 + 2:k0 + 3, :]
    parts = _row_halves(h_ref.shape[0])
    u = [(h_ref[r, :] * (1.0 + scale) + shift).astype(BF16) for r in parts]
    act = []
    for uu in u:
        g = _dot(uu, wg_ref[...])
        up = _dot(uu, wu_ref[...])
        act.append((_silu(g) * up).astype(BF16))
    y = [_dot(a, wd_ref[...]) for a in act]
    for r, yy in zip(parts, y):
        res = DN_ALPHA * h_ref[r, :] + (0.5 * (1.0 + gate)) * yy
        o_ref[r, :] = _layer_norm(res, lng_ref[...], lnb_ref[...])


def _ffn_call(h, ada_l, wg, wu, wd, lng, lnb, *, k0, seq, tm):
    tokens = h.shape[0]
    per_b = seq // tm
    row = pl.BlockSpec((tm, D_MODEL), lambda i: (i, 0))
    return pl.pallas_call(
        functools.partial(_ffn_kernel, k0=k0),
        grid=(tokens // tm,),
        in_specs=[
            row,
            pl.BlockSpec((1, N_ADA, D_MODEL), lambda i: (i // per_b, 0, 0)),
            _resident((D_MODEL, D_FF)),
            _resident((D_MODEL, D_FF)),
            _resident((D_FF, D_MODEL)),
            _resident((1, D_MODEL)),
            _resident((1, D_MODEL)),
        ],
        out_specs=row,
        out_shape=jax.ShapeDtypeStruct((tokens, D_MODEL), F32),
        compiler_params=_params("arbitrary"),
        name="ffn",
    )(h, ada_l, wg, wu, wd, lng, lnb)


def _inproj_kernel(h_ref, ada_ref, whg_ref, wlat_ref, wgate_ref, lbraw_ref, qng_ref, wuq_ref,
                   kvng_ref, wuk_ref, wvt_ref, cos_ref, sin_ref,
                   hq_ref, lf_ref, hk_ref, hv_ref, og_ref, q_ref, k_ref, vt_ref, sga_ref, sgb_ref,
                   *, layer):
    h = h_ref[...]
    shift = ada_ref[0, 3:4, :]
    scale = ada_ref[0, 4:5, :]
    u = (h * (1.0 + scale) + shift).astype(BF16)

    raw = lbraw_ref[...]
    e = jnp.exp(raw - jnp.max(raw, axis=0, keepdims=True))
    lb = jnp.zeros((1, HEADS * HG_DK), F32)
    for i in range(1, layer + 1):
        lb = lb + e[i:i + 1, :]
    lb = lb / jnp.sum(e, axis=0, keepdims=True)
    one_minus_lb = 1.0 - lb

    n = HEADS * HG_DK
    zf = _dot(u, whg_ref[:, n:2 * n])
    t = jnp.exp(-jnp.abs(zf))
    r = 1.0 / (1.0 + t)
    tr = t * r
    pos = zf >= 0.0
    sig_pos = jnp.where(pos, r, tr)
    sig_neg = jnp.where(pos, tr, r)
    lf_ref[...] = jnp.log2(jnp.maximum(lb, LB_FLOOR) + one_minus_lb * sig_pos)
    hk_ref[...] = (one_minus_lb * sig_neg).astype(BF16)

    zgate = _dot(u, wgate_ref[...])
    sga_ref[...] = jax.nn.sigmoid(zgate[:, 0:D_MODEL]).astype(sga_ref.dtype)
    sgb_ref[...] = jax.nn.sigmoid(zgate[:, D_MODEL:2 * D_MODEL]).astype(sgb_ref.dtype)

    zl = _dot(u, wlat_ref[...])
    cos_t = cos_ref[...]
    sin_t = sin_ref[...]
    cqn = _rms_norm(zl[:, 0:Q_LORA], qng_ref[...]).astype(BF16)
    qf = _dot(cqn, wuq_ref[...]) * ATTN_SCALE_LOG2E
    nq = HEADS * MLA_NOPE
    nr = HEADS * MLA_ROPE
    cos_q = jnp.concatenate([cos_t] * (nr // 128), axis=1)
    sin_q = jnp.concatenate([sin_t] * (nr // 128), axis=1)
    q_rope = qf[:, nq:nq + nr] * cos_q + qf[:, nq + nr:nq + 2 * nr] * sin_q
    ckvn = _rms_norm(zl[:, Q_LORA:Q_LORA + KV_LORA], kvng_ref[...]).astype(BF16)
    k_nope = _dot(ckvn, wuk_ref[...])
    vt_ref[...] = _dot_nt(wvt_ref[...], ckvn).astype(BF16)
    c0 = Q_LORA + KV_LORA
    k_rope = (zl[:, c0:c0 + 128] * cos_t + zl[:, c0 + 128:c0 + 256] * sin_t)[:, 0:MLA_ROPE]
    k_rope = k_rope.astype(BF16)
    for hh in range(HEADS):
        q_ref[hh, :, 0:MLA_NOPE] = qf[:, hh * MLA_NOPE:(hh + 1) * MLA_NOPE].astype(BF16)
        q_ref[hh, :, MLA_NOPE:MLA_QK] = q_rope[:, hh * MLA_ROPE:(hh + 1) * MLA_ROPE].astype(BF16)
        k_ref[hh, :, 0:MLA_NOPE] = k_nope[:, hh * MLA_NOPE:(hh + 1) * MLA_NOPE].astype(BF16)
        k_ref[hh, :, MLA_NOPE:MLA_QK] = k_rope

    hq_ref[...] = _silu(_dot(u, whg_ref[:, 0:n])).astype(BF16)
    og_ref[...] = _silu(_dot(u, whg_ref[:, 3 * n:4 * n])).astype(og_ref.dtype)
    hv_ref[...] = _dot(u, whg_ref[:, 2 * n:3 * n]).astype(BF16)


def _inproj_call(h, ada_l, w, lbraw, cos_t, sin_t, *, layer, seq, tm):
    tokens = h.shape[0]
    per_b = seq // tm
    row = pl.BlockSpec((tm, D_MODEL), lambda i: (i, 0))
    tab = pl.BlockSpec((tm, 128), lambda i: (i, 0))
    headed = pl.BlockSpec((HEADS, tm, MLA_QK), lambda i: (0, i, 0))
    wide = jax.ShapeDtypeStruct((tokens, D_MODEL), BF16)
    qk = jax.ShapeDtypeStruct((HEADS, tokens, MLA_QK), BF16)
    return pl.pallas_call(
        functools.partial(_inproj_kernel, layer=layer),
        grid=(tokens // tm,),
        in_specs=[
            row,
            pl.BlockSpec((1, N_ADA, D_MODEL), lambda i: (i // per_b, 0, 0)),
            _resident(w["w_hg"].shape),
            _resident(w["w_lat"].shape),
            _resident(w["w_gate"].shape),
            _resident(lbraw.shape),
            _resident((1, Q_LORA)),
            _resident(w["w_uq"].shape),
            _resident((1, KV_LORA)),
            _resident(w["w_uk"].shape),
            _resident(w["w_vt"].shape),
            tab, tab,
        ],
        out_specs=[row, row, row, row, row, headed, headed,
                   pl.BlockSpec((HEADS * MLA_V, tm), lambda i: (0, i)), row, row],
        out_shape=[wide, jax.ShapeDtypeStruct((tokens, D_MODEL), F32), wide, wide, wide,
                   qk, qk, jax.ShapeDtypeStruct((HEADS * MLA_V, tokens), BF16), wide, wide],
        compiler_params=_params("arbitrary"),
        name="inproj",
    )(h, ada_l, w["w_hg"], w["w_lat"], w["w_gate"], lbraw, w["q_norm_g"], w["w_uq"],
      w["kv_norm_g"], w["w_uk"], w["w_vt"], cos_t, sin_t)


def _hgrn_masks():
    t = np.arange(HG_CHUNK)[:, None]
    s = np.arange(HG_CHUNK)[None, :]
    masks = [(t == s)]
    m = 1
    while m < HG_CHUNK:
        masks.append((t // (2 * m) == s // (2 * m)) & ((t & m) != 0) & ((s & m) == 0))
        m *= 2
    return np.stack(masks).astype(np.float32)


def _hgrn_kernel(q_ref, lf_ref, k_ref, v_ref, og_ref, gn_ref, mask_ref, o_ref, state_ref, *, n_chunks):
    @pl.when(pl.program_id(1) == 0)
    def _():
        state_ref[...] = jnp.zeros_like(state_ref)

    n_grp = HG_CHUNK // SUBLANES
    sub = lax.broadcasted_iota(jnp.int32, (SUBLANES, HG_DK), 0)
    sign_small = {m: jnp.where((sub & m) != 0, 1.0, -1.0) for m in (1, 2, 4)}

    def bcast_row(x, i):
        return jnp.broadcast_to(x[i:i + 1, :], (SUBLANES, HG_DK))

    def prepare(rows, hh):
        cols = slice(hh * HG_DK, (hh + 1) * HG_DK)
        q_bf = q_ref[rows, cols]
        k_bf = k_ref[rows, cols]
        q = q_bf.astype(F32)
        k = k_bf.astype(F32)

        lf = lf_ref[rows, cols]
        grp = [lf[SUBLANES * r:SUBLANES * (r + 1)] for r in range(n_grp)]
        for sh in (1, 2, 4):
            grp = [p + jnp.where(sub >= sh, pltpu.roll(p, sh, axis=0), 0.0) for p in grp]
        g_grp, ends = [], []
        for r in range(n_grp):
            gp = grp[r] if r == 0 else grp[r] + ends[r - 1]
            g_grp.append(gp)
            ends.append(bcast_row(gp, SUBLANES - 1))
        q_dec = (q * jnp.exp2(jnp.concatenate(g_grp, axis=0))).astype(BF16)
        diag = _dot_nt(q_bf, k_bf) * mask_ref[0]
        return dict(q=q, k=k, g_grp=g_grp, ends=ends, q_dec=q_dec, scores=diag)

    def level(hd, m, lvl):
        g_grp, ends = hd["g_grp"], hd["ends"]
        if m >= SUBLANES:
            gm = m // SUBLANES
            refs = [ends[(r // (2 * gm)) * 2 * gm + gm - 1] for r in range(n_grp)]
            args = [gp - rf if (SUBLANES * r) & m else rf - gp
                    for r, (gp, rf) in enumerate(zip(g_grp, refs))]
        else:
            if m == 4:
                refs = [bcast_row(gp, 3) for gp in g_grp]
            elif m == 2:
                refs = [jnp.where(sub < 4, bcast_row(gp, 1), bcast_row(gp, 5)) for gp in g_grp]
            else:
                refs = [jnp.where((sub & 1) != 0, pltpu.roll(gp, 1, axis=0), gp) for gp in g_grp]
            args = [(gp - rf) * sign_small[m] for gp, rf in zip(g_grp, refs)]
        if m >= SUBLANES:
            zero = jnp.zeros((SUBLANES, HG_DK), F32)
            q_rows, k_rows = [], []
            for r, a in enumerate(args):
                rows_r = slice(SUBLANES * r, SUBLANES * (r + 1))
                e_r = jnp.exp2(a)
                is_query = bool((SUBLANES * r) & m)
                q_rows.append(hd["q"][rows_r] * e_r if is_query else zero)
                k_rows.append(zero if is_query else hd["k"][rows_r] * e_r)
            q_side = jnp.concatenate(q_rows, axis=0)
            k_side = jnp.concatenate(k_rows, axis=0)
        else:
            e = jnp.exp2(jnp.concatenate(args, axis=0))
            q_side = hd["q"] * e
            k_side = hd["k"] * e
        part = _dot_nt(q_side.astype(BF16), k_side.astype(BF16))
        if 2 * m == HG_CHUNK:
            hd["scores"] = hd["scores"] + part
        else:
            hd["scores"] = hd["scores"] + part * mask_ref[lvl]

    def finish(rows, hh, hd):
        cols = slice(hh * HG_DK, (hh + 1) * HG_DK)
        v = v_ref[rows, cols]
        state = state_ref[hh]
        o = _dot_nt(hd["q_dec"], state.astype(BF16)) + _dot(hd["scores"].astype(BF16), v)
        g_end = hd["ends"][n_grp - 1]
        to_end = jnp.exp2(jnp.concatenate([g_end - gp for gp in hd["g_grp"]], axis=0))
        k_dec = (hd["k"] * to_end).astype(BF16)
        state_ref[hh] = jnp.exp2(g_end[0:1, :]) * state + _dot_tn(v, k_dec)
        y = _rms_norm(o, gn_ref[:, cols]) * og_ref[rows, cols].astype(F32)
        o_ref[rows, cols] = y.astype(o_ref.dtype)

    def chunk_pair(c, carry):
        rows = [pl.ds(pl.multiple_of((2 * c + i) * HG_CHUNK, HG_CHUNK), HG_CHUNK) for i in range(2)]
        work = [(r, hh, prepare(r, hh)) for r in rows for hh in range(HEADS)]
        m, lvl = 1, 1
        while m < HG_CHUNK:
            for _, _, hd in work:
                level(hd, m, lvl)
            m *= 2
            lvl += 1
        for r, hh, hd in work:
            finish(r, hh, hd)
        return carry

    lax.fori_loop(0, n_chunks // 2, chunk_pair, 0)


def _hgrn_call(hq, lf, hk, hv, og, gnorm, *, batch, seq, tb):
    tokens = hq.shape[0]
    per_b = seq // tb
    width = HEADS * HG_DK
    blk = pl.BlockSpec((tb, width), lambda b, i: (b * per_b + i, 0))
    masks = jnp.asarray(_hgrn_masks())
    return pl.pallas_call(
        functools.partial(_hgrn_kernel, n_chunks=tb // HG_CHUNK),
        grid=(batch, per_b),
        in_specs=[blk, blk, blk, blk, blk, _resident((1, width)), _resident(masks.shape)],
        out_specs=blk,
        out_shape=jax.ShapeDtypeStruct((tokens, width), BF16),
        scratch_shapes=[pltpu.VMEM((HEADS, HG_DV, HG_DK), F32)],
        compiler_params=_params("arbitrary", "arbitrary"),
        name="hgrn",
    )(hq, lf, hk, hv, og, gnorm.reshape(1, width), masks)


def _attn_kernel(q_ref, k_ref, vt_ref, o_ref, m_ref, acc_ref, s_ref, bmax_ref, *, tq, hps):
    qi = pl.program_id(2)
    m_ref[...] = jnp.full_like(m_ref, -jnp.inf)
    acc_ref[...] = jnp.zeros_like(acc_ref)
    ones = jnp.ones((ONES_ROWS, tq), BF16)

    def scores(a, j, slot):
        start = pl.multiple_of(j * tq, tq)
        s_t = _dot_nt(k_ref[a, pl.ds(start, tq), :], q_ref[a])
        s_ref[a, slot] = s_t
        bmax_ref[a, slot] = jnp.max(s_t, axis=0, keepdims=True)

    def accumulate(a, j, slot, causal):
        start = pl.multiple_of(j * tq, tq)
        s_t = s_ref[a, slot]
        if causal:
            kpos = lax.broadcasted_iota(jnp.int32, (tq, tq), 0)
            qpos = lax.broadcasted_iota(jnp.int32, (tq, tq), 1)
            s_t = jnp.where(kpos <= qpos, s_t, MASK_VALUE)
            block_max = jnp.max(s_t, axis=0, keepdims=True)
        else:
            block_max = bmax_ref[a, slot]
        m_old = m_ref[a]
        m_new = jnp.maximum(m_old, block_max)
        p = jnp.exp2((s_t - m_new).astype(BF16))
        alpha = jnp.exp2(m_old - m_new)
        v_ext = jnp.concatenate([vt_ref[a * MLA_V:(a + 1) * MLA_V, pl.ds(start, tq)], ones], axis=0)
        acc_ref[a] = alpha * acc_ref[a] + _dot(v_ext, p)
        m_ref[a] = m_new

    for a in range(hps):
        scores(a, 0, 0)

    def step(j, slot):
        for a in range(hps):
            scores(a, j + 1, 1 - slot)
            accumulate(a, j, slot, False)

    def body(jj, carry):
        step(2 * jj, 0)
        step(2 * jj + 1, 1)
        return carry

    lax.fori_loop(0, qi // 2, body, 0)

    @pl.when(qi % 2 == 1)
    def _():
        step(qi - 1, 0)
        for a in range(hps):
            accumulate(a, qi, 1, True)

    @pl.when(qi % 2 == 0)
    def _():
        for a in range(hps):
            accumulate(a, qi, 0, True)

    for a in range(hps):
        acc = acc_ref[a]
        o_t = acc[0:MLA_V] * (1.0 / acc[MLA_V:MLA_V + 1])
        o_ref[:, a * MLA_V:(a + 1) * MLA_V] = o_t.T.astype(o_ref.dtype)


def _attn_call(q, k, vt, *, batch, seq, tq, hps):
    tokens = vt.shape[1]
    nq = seq // tq
    return pl.pallas_call(
        functools.partial(_attn_kernel, tq=tq, hps=hps),
        grid=(batch, HEADS // hps, nq),
        in_specs=[
            pl.BlockSpec((hps, tq, MLA_QK), lambda b, g, i: (g, b * nq + i, 0)),
            pl.BlockSpec((hps, seq, MLA_QK), lambda b, g, i: (g, b, 0),
                         pipeline_mode=pl.Buffered(1)),
            pl.BlockSpec((hps * MLA_V, seq), lambda b, g, i: (g, b),
                         pipeline_mode=pl.Buffered(1)),
        ],
        out_specs=pl.BlockSpec((tq, hps * MLA_V), lambda b, g, i: (b * nq + i, g)),
        out_shape=jax.ShapeDtypeStruct((tokens, HEADS * MLA_V), BF16),
        scratch_shapes=[pltpu.VMEM((hps, 1, tq), F32),
                        pltpu.VMEM((hps, MLA_V + ONES_ROWS, tq), F32),
                        pltpu.VMEM((hps, 2, tq, tq), F32),
                        pltpu.VMEM((hps, 2, 1, tq), F32)],
        compiler_params=_params("arbitrary", "arbitrary", "arbitrary"),
        name="mla_attn",
    )(q, k, vt)


def _merge_kernel(h_ref, ada_ref, yh_ref, om_ref, sga_ref, sgb_ref, wbh_ref, wbm_ref, wo_ref,
                  lng_ref, lnb_ref, o_ref):
    gate = ada_ref[0, 5:6, :]
    parts = _row_halves(h_ref.shape[0])
    y_hg = [_dot(yh_ref[r, :], wbh_ref[...]) for r in parts]
    y_mla = [_dot(om_ref[r, :], wbm_ref[...]) for r in parts]
    merged = [(sga_ref[r, :].astype(F32) * a + sgb_ref[r, :].astype(F32) * b).astype(BF16)
              for r, a, b in zip(parts, y_hg, y_mla)]
    y = [_dot(m, wo_ref[...]) for m in merged]
    for r, yy in zip(parts, y):
        res = DN_ALPHA * h_ref[r, :] + (1.0 + gate) * yy
        o_ref[r, :] = _layer_norm(res, lng_ref[...], lnb_ref[...])


def _merge_call(h, ada_l, yh, om, sga, sgb, wbh, wbm, wo, lng, lnb, *, seq, tm):
    tokens = h.shape[0]
    per_b = seq // tm
    row = pl.BlockSpec((tm, D_MODEL), lambda i: (i, 0))
    sq = _resident((D_MODEL, D_MODEL))
    return pl.pallas_call(
        _merge_kernel,
        grid=(tokens // tm,),
        in_specs=[row, pl.BlockSpec((1, N_ADA, D_MODEL), lambda i: (i // per_b, 0, 0)),
                  row, row, row, row, sq, sq, sq,
                  _resident((1, D_MODEL)), _resident((1, D_MODEL))],
        out_specs=row,
        out_shape=jax.ShapeDtypeStruct((tokens, D_MODEL), F32),
        compiler_params=_params("arbitrary"),
        name="merge_out",
    )(h, ada_l, yh, om, sga, sgb, wbh, wbm, wo, lng, lnb)


def _mixer_weights(w_in, w_uq, w_ukv, q_norm_g, kv_norm_g):
    n = HEADS * HG_DK
    c_cq = 4 * n
    c_ckv = c_cq + Q_LORA
    c_kr = c_ckv + KV_LORA
    c_ga = c_kr + MLA_ROPE
    half = MLA_ROPE // 2
    kr = w_in[:, c_kr:c_ga]
    kr_swapped = jnp.concatenate([kr[:, half:], kr[:, :half]], axis=1)
    pad = jnp.zeros((D_MODEL, 128 - MLA_ROPE), w_in.dtype)
    w_lat = jnp.concatenate([w_in[:, c_cq:c_kr], kr, pad, kr_swapped, pad], axis=1)

    uq = w_uq.reshape(Q_LORA, HEADS, MLA_QK)
    uq_nope = uq[:, :, :MLA_NOPE].reshape(Q_LORA, HEADS * MLA_NOPE)
    uq_rope = uq[:, :, MLA_NOPE:]
    uq_rope_sw = jnp.concatenate([uq_rope[:, :, half:], uq_rope[:, :, :half]], axis=2)
    w_uq_p = jnp.concatenate([uq_nope, uq_rope.reshape(Q_LORA, HEADS * MLA_ROPE),
                              uq_rope_sw.reshape(Q_LORA, HEADS * MLA_ROPE)], axis=1)

    ukv = w_ukv.reshape(KV_LORA, HEADS, MLA_NOPE + MLA_V)
    w_uk = ukv[:, :, :MLA_NOPE].reshape(KV_LORA, HEADS * MLA_NOPE)
    w_vt = ukv[:, :, MLA_NOPE:].reshape(KV_LORA, HEADS * MLA_V).T
    return {
        "w_hg": w_in[:, :c_cq].astype(BF16),
        "w_lat": w_lat.astype(BF16),
        "w_gate": w_in[:, c_ga:].astype(BF16),
        "w_uq": w_uq_p.astype(BF16),
        "w_uk": w_uk.astype(BF16),
        "w_vt": w_vt.astype(BF16),
        "q_norm_g": q_norm_g.reshape(1, Q_LORA),
        "kv_norm_g": kv_norm_g.reshape(1, KV_LORA),
    }


def _tile(seq, want):
    t = min(want, seq)
    assert seq % t == 0, (seq, t)
    return t


def kernel(x, c, positions, ada_w, ada_b, ln_g, ln_b, ffn1_gate, ffn1_up, ffn1_down, w_in,
           hg_lower_bound, hg_norm_g, mla_q_norm_g, mla_w_uq, mla_kv_norm_g, mla_w_ukv,
           w_branch_hg, w_branch_mla, w_out, ffn2_gate, ffn2_up, ffn2_down):
    batch, seq, _ = x.shape
    tokens = batch * seq
    tm_ffn = _tile(seq, 512)
    tm_in = _tile(seq, 256)
    tm_merge = _tile(seq, 512)
    tb_hgrn = _tile(seq, 512)
    tq = _tile(seq, 512)
    attn_heads_per_step = 4

    ada = _ada_call(c, ada_w, ada_b).reshape(DEPTH, batch, N_ADA, D_MODEL)
    cos_t, sin_t = _rope_call(positions)
    lbraw = hg_lower_bound.astype(F32)

    h = x.reshape(tokens, D_MODEL)
    for l in range(DEPTH):
        ada_l = ada[l]
        ln = lambda i: (ln_g[l, i].reshape(1, D_MODEL), ln_b[l, i].reshape(1, D_MODEL))

        h = _ffn_call(h, ada_l, ffn1_gate[l].astype(BF16), ffn1_up[l].astype(BF16),
                      ffn1_down[l].astype(BF16), *ln(0), k0=0, seq=seq, tm=tm_ffn)

        w = _mixer_weights(w_in[l], mla_w_uq[l], mla_w_ukv[l], mla_q_norm_g[l], mla_kv_norm_g[l])
        hq, lf, hk, hv, og, q, k, vt, sga, sgb = _inproj_call(
            h, ada_l, w, lbraw, cos_t, sin_t, layer=l, seq=seq, tm=tm_in)
        yh = _hgrn_call(hq, lf, hk, hv, og, hg_norm_g[l], batch=batch, seq=seq, tb=tb_hgrn)
        om = _attn_call(q, k, vt, batch=batch, seq=seq, tq=tq, hps=attn_heads_per_step)
        h = _merge_call(h, ada_l, yh, om, sga, sgb, w_branch_hg[l].astype(BF16),
                        w_branch_mla[l].astype(BF16), w_out[l].astype(BF16), *ln(1),
                        seq=seq, tm=tm_merge)

        h = _ffn_call(h, ada_l, ffn2_gate[l].astype(BF16), ffn2_up[l].astype(BF16),
                      ffn2_down[l].astype(BF16), *ln(2), k0=6, seq=seq, tm=tm_ffn)
    return h.reshape(batch, seq, D_MODEL)
```

```python
import functools

import jax
import jax.numpy as jnp
import numpy as np
from jax import lax
from jax.experimental import pallas as pl
from jax.experimental.pallas import tpu as pltpu

F32 = jnp.float32
BF16 = jnp.bfloat16

D_MODEL = 1024
DEPTH = 4
HEADS = 8
HG_DK = 128
HG_DV = 128
HG_CHUNK = 64
LB_FLOOR = 1e-30
MLA_NOPE = 128
MLA_ROPE = 64
MLA_QK = MLA_NOPE + MLA_ROPE
MLA_V = 128
Q_LORA = 384
KV_LORA = 256
ROPE_THETA = 10000.0
MASK_VALUE = -1e30
D_FF = 2816
DN_ALPHA = (2.0 * DEPTH) ** 0.25
LN_EPS = 1e-5
RMS_EPS = 1e-6
N_ADA = 9
LOG2E = 1.4426950408889634
ATTN_SCALE_LOG2E = float(MLA_QK) ** -0.5 * LOG2E
SUBLANES = 8
ONES_ROWS = 16
FFN_ROW_GROUPS = 4
HG_CHUNKS_PER_ITER = 8

VMEM_LIMIT_BYTES = 56 * 1024 * 1024


def _params(*sem):
    return pltpu.CompilerParams(dimension_semantics=sem, vmem_limit_bytes=VMEM_LIMIT_BYTES)


def _resident(shape):
    zeros = (0,) * len(shape)
    return pl.BlockSpec(shape, lambda *_: zeros, pipeline_mode=pl.Buffered(1))


def _stacked(arr, index):
    rest = (0,) * (arr.ndim - 1)
    return pl.BlockSpec((None,) + arr.shape[1:], lambda *_: (index,) + rest,
                        pipeline_mode=pl.Buffered(1))


def _silu(x):
    return x * jax.nn.sigmoid(x)


def _dot(a, b):
    return jnp.dot(a, b, preferred_element_type=F32)


def _dot_nt(a, b):
    return lax.dot_general(a, b, (((1,), (1,)), ((), ())), preferred_element_type=F32)


def _dot_tn(a, b):
    return lax.dot_general(a, b, (((0,), (0,)), ((), ())), preferred_element_type=F32)


def _row_parts(rows, n):
    assert rows % n == 0, (rows, n)
    return [slice(i * rows // n, (i + 1) * rows // n) for i in range(n)]


def _layer_norm(r, g, b):
    mu = jnp.mean(r, axis=-1, keepdims=True)
    d = r - mu
    var = jnp.mean(d * d, axis=-1, keepdims=True)
    return d * lax.rsqrt(var + LN_EPS) * g + b


def _rms_norm(x, g):
    return x * lax.rsqrt(jnp.mean(x * x, axis=-1, keepdims=True) + RMS_EPS) * g


def _ada_kernel(c_ref, w_ref, b_ref, o_ref):
    cond = _silu(c_ref[...])
    o_ref[0] = jnp.dot(cond, w_ref[0], preferred_element_type=F32,
                       precision=lax.Precision.HIGHEST) + b_ref[0]


def _ada_call(c, ada_w, ada_b):
    batch = c.shape[0]
    n_out = ada_w.shape[-1]
    tn = D_MODEL
    return pl.pallas_call(
        _ada_kernel,
        grid=(DEPTH, n_out // tn),
        in_specs=[
            pl.BlockSpec((batch, D_MODEL), lambda l, j: (0, 0)),
            pl.BlockSpec((1, D_MODEL, tn), lambda l, j: (l, 0, j)),
            pl.BlockSpec((1, 1, tn), lambda l, j: (l, 0, j)),
        ],
        out_specs=pl.BlockSpec((1, batch, tn), lambda l, j: (l, 0, j)),
        out_shape=jax.ShapeDtypeStruct((DEPTH, batch, n_out), F32),
        compiler_params=_params("arbitrary", "arbitrary"),
        name="ada",
    )(c, ada_w, ada_b.reshape(DEPTH, 1, n_out))


def _rope_kernel(pos_ref, inv_ref, sign_ref, cos_ref, sin_ref):
    ang = pos_ref[...].astype(F32) * inv_ref[...]
    cos_ref[...] = jnp.cos(ang)
    sin_ref[...] = jnp.sin(ang) * sign_ref[...]


def _rope_call(positions):
    tokens = positions.size
    tm = min(1024, tokens)
    half = MLA_ROPE // 2
    inv = 1.0 / (ROPE_THETA ** (jnp.arange(0, MLA_ROPE, 2, dtype=F32) / MLA_ROPE))
    inv_t = jnp.tile(inv, 4).reshape(1, 4 * half)
    sign = jnp.tile(jnp.concatenate([-jnp.ones((half,), F32), jnp.ones((half,), F32)]), 2)
    sign = sign.reshape(1, 4 * half)
    row = pl.BlockSpec((tm, 4 * half), lambda i: (i, 0))
    const = pl.BlockSpec((1, 4 * half), lambda i: (0, 0))
    return pl.pallas_call(
        _rope_kernel,
        grid=(tokens // tm,),
        in_specs=[pl.BlockSpec((tm, 1), lambda i: (i, 0)), const, const],
        out_specs=[row, row],
        out_shape=[jax.ShapeDtypeStruct((tokens, 4 * half), F32)] * 2,
        compiler_params=_params("arbitrary"),
        name="rope_tables",
    )(positions.reshape(tokens, 1), inv_t, sign)


def _ffn_kernel(h_ref, ada_ref, wg_ref, wu_ref, wd_ref, lng_ref, lnb_ref, o_ref, *, k0):
    shift = ada_ref[0, k0:k0 + 1, :]
    scale = ada_ref[0, k0 + 1:k0 + 2, :]
    gate = ada_ref[0, k0 + 2:k0 + 3, :]
    parts = _row_parts(h_ref.shape[0], FFN_ROW_GROUPS)
    u = [(h_ref[r, :] * (1.0 + scale) + shift).astype(BF16) for r in parts]
    act = []
    for uu in u:
        g = _dot(uu, wg_ref[...])
        up = _dot(uu, wu_ref[...])
        act.append((_silu(g) * up).astype(BF16))
    y = [_dot(a, wd_ref[...]) for a in act]
    for r, yy in zip(parts, y):
        res = DN_ALPHA * h_ref[r, :] + (0.5 * (1.0 + gate)) * yy
        o_ref[r, :] = _layer_norm(res, lng_ref[...], lnb_ref[...])


def _ada_spec(layer, per_b):
    return pl.BlockSpec((None, 1, N_ADA, D_MODEL), lambda i: (layer, i // per_b, 0, 0))


def _ffn_call(h, ada, wg, wu, wd, ln_g, ln_b, *, layer, sub, seq, tm):
    tokens = h.shape[0]
    row = pl.BlockSpec((tm, D_MODEL), lambda i: (i, 0))
    return pl.pallas_call(
        functools.partial(_ffn_kernel, k0=3 * sub),
        grid=(tokens // tm,),
        in_specs=[
            row,
            _ada_spec(layer, seq // tm),
            _stacked(wg, layer),
            _stacked(wu, layer),
            _stacked(wd, layer),
            _stacked(ln_g, 3 * layer + sub),
            _stacked(ln_b, 3 * layer + sub),
        ],
        out_specs=row,
        out_shape=jax.ShapeDtypeStruct((tokens, D_MODEL), F32),
        compiler_params=_params("arbitrary"),
        name="ffn",
    )(h, ada, wg, wu, wd, ln_g, ln_b)


def _inproj_kernel(h_ref, ada_ref, whg_ref, wlat_ref, wgate_ref, lbraw_ref, qng_ref, wuq_ref,
                   kvng_ref, wuk_ref, wvt_ref, cos_ref, sin_ref,
                   hq_ref, lf_ref, hk_ref, hv_ref, og_ref, q_ref, k_ref, vt_ref, sga_ref, sgb_ref,
                   *, layer):
    shift = ada_ref[0, 3:4, :]
    scale = ada_ref[0, 4:5, :]

    raw = lbraw_ref[...]
    e = jnp.exp(raw - jnp.max(raw, axis=0, keepdims=True))
    lb = jnp.zeros((1, HEADS * HG_DK), F32)
    for i in range(1, layer + 1):
        lb = lb + e[i:i + 1, :]
    lb = lb / jnp.sum(e, axis=0, keepdims=True)
    one_minus_lb = 1.0 - lb

    def project(rows):
        u = (h_ref[rows, :] * (1.0 + scale) + shift).astype(BF16)
        n = HEADS * HG_DK
        zf = _dot(u, whg_ref[:, n:2 * n])
        t = jnp.exp(-jnp.abs(zf))
        r = 1.0 / (1.0 + t)
        tr = t * r
        pos = zf >= 0.0
        sig_pos = jnp.where(pos, r, tr)
        sig_neg = jnp.where(pos, tr, r)
        lf_ref[rows, :] = jnp.log2(jnp.maximum(lb, LB_FLOOR) + one_minus_lb * sig_pos)
        hk_ref[rows, :] = (one_minus_lb * sig_neg).astype(BF16)

        zgate = _dot(u, wgate_ref[...])
        sga_ref[rows, :] = jax.nn.sigmoid(zgate[:, 0:D_MODEL]).astype(sga_ref.dtype)
        sgb_ref[rows, :] = jax.nn.sigmoid(zgate[:, D_MODEL:2 * D_MODEL]).astype(sgb_ref.dtype)

        zl = _dot(u, wlat_ref[...])
        cos_t = cos_ref[rows, :]
        sin_t = sin_ref[rows, :]
        cqn = _rms_norm(zl[:, 0:Q_LORA], qng_ref[...]).astype(BF16)
        qf = _dot(cqn, wuq_ref[...]) * ATTN_SCALE_LOG2E
        nq = HEADS * MLA_NOPE
        nr = HEADS * MLA_ROPE
        cos_q = jnp.concatenate([cos_t] * (nr // 128), axis=1)
        sin_q = jnp.concatenate([sin_t] * (nr // 128), axis=1)
        q_rope = qf[:, nq:nq + nr] * cos_q + qf[:, nq + nr:nq + 2 * nr] * sin_q
        ckvn = _rms_norm(zl[:, Q_LORA:Q_LORA + KV_LORA], kvng_ref[...]).astype(BF16)
        k_nope = _dot(ckvn, wuk_ref[...])
        vt_ref[:, rows] = _dot_nt(wvt_ref[...], ckvn).astype(BF16)
        c0 = Q_LORA + KV_LORA
        k_rope = (zl[:, c0:c0 + 128] * cos_t + zl[:, c0 + 128:c0 + 256] * sin_t)[:, 0:MLA_ROPE]
        k_rope = k_rope.astype(BF16)
        for hh in range(HEADS):
            q_ref[hh, rows, 0:MLA_NOPE] = qf[:, hh * MLA_NOPE:(hh + 1) * MLA_NOPE].astype(BF16)
            q_ref[hh, rows, MLA_NOPE:MLA_QK] = (
                q_rope[:, hh * MLA_ROPE:(hh + 1) * MLA_ROPE].astype(BF16))
            k_ref[hh, rows, 0:MLA_NOPE] = k_nope[:, hh * MLA_NOPE:(hh + 1) * MLA_NOPE].astype(BF16)
            k_ref[hh, rows, MLA_NOPE:MLA_QK] = k_rope

        hq_ref[rows, :] = _silu(_dot(u, whg_ref[:, 0:n])).astype(BF16)
        og_ref[rows, :] = _silu(_dot(u, whg_ref[:, 3 * n:4 * n])).astype(og_ref.dtype)
        hv_ref[rows, :] = _dot(u, whg_ref[:, 2 * n:3 * n]).astype(BF16)

    for rows in _row_parts(h_ref.shape[0], 2):
        project(rows)


def _inproj_call(h, ada, w, lbraw, cos_t, sin_t, *, layer, seq, tm):
    tokens = h.shape[0]
    row = pl.BlockSpec((tm, D_MODEL), lambda i: (i, 0))
    tab = pl.BlockSpec((tm, 128), lambda i: (i, 0))
    headed = pl.BlockSpec((HEADS, tm, MLA_QK), lambda i: (0, i, 0))
    wide = jax.ShapeDtypeStruct((tokens, D_MODEL), BF16)
    qk = jax.ShapeDtypeStruct((HEADS, tokens, MLA_QK), BF16)
    return pl.pallas_call(
        functools.partial(_inproj_kernel, layer=layer),
        grid=(tokens // tm,),
        in_specs=[
            row,
            _ada_spec(layer, seq // tm),
            _stacked(w["w_hg"], layer),
            _stacked(w["w_lat"], layer),
            _stacked(w["w_gate"], layer),
            _resident(lbraw.shape),
            _stacked(w["q_norm_g"], layer),
            _stacked(w["w_uq"], layer),
            _stacked(w["kv_norm_g"], layer),
            _stacked(w["w_uk"], layer),
            _stacked(w["w_vt"], layer),
            tab, tab,
        ],
        out_specs=[row, row, row, row, row, headed, headed,
                   pl.BlockSpec((HEADS * MLA_V, tm), lambda i: (0, i)), row, row],
        out_shape=[wide, jax.ShapeDtypeStruct((tokens, D_MODEL), F32), wide, wide, wide,
                   qk, qk, jax.ShapeDtypeStruct((HEADS * MLA_V, tokens), BF16), wide, wide],
        compiler_params=_params("arbitrary"),
        name="inproj",
    )(h, ada, w["w_hg"], w["w_lat"], w["w_gate"], lbraw, w["q_norm_g"], w["w_uq"],
      w["kv_norm_g"], w["w_uk"], w["w_vt"], cos_t, sin_t)


def _hgrn_masks():
    t = np.arange(HG_CHUNK)[:, None]
    s = np.arange(HG_CHUNK)[None, :]
    masks = [(t == s)]
    m = 1
    while m < HG_CHUNK:
        masks.append((t // (2 * m) == s // (2 * m)) & ((t & m) != 0) & ((s & m) == 0))
        m *= 2
    return np.stack(masks).astype(np.float32)


def _hgrn_kernel(q_ref, lf_ref, k_ref, v_ref, og_ref, gn_ref, mask_ref, o_ref, state_ref, *, n_chunks):
    @pl.when(pl.program_id(1) == 0)
    def _():
        state_ref[...] = jnp.zeros_like(state_ref)

    n_grp = HG_CHUNK // SUBLANES
    sub = lax.broadcasted_iota(jnp.int32, (SUBLANES, HG_DK), 0)
    sign_small = {m: jnp.where((sub & m) != 0, 1.0, -1.0) for m in (1, 2, 4)}

    def bcast_row(x, i):
        return jnp.broadcast_to(x[i:i + 1, :], (SUBLANES, HG_DK))

    def prepare(rows, hh):
        cols = slice(hh * HG_DK, (hh + 1) * HG_DK)
        q_bf = q_ref[rows, cols]
        k_bf = k_ref[rows, cols]
        q = q_bf.astype(F32)
        k = k_bf.astype(F32)

        lf = lf_ref[rows, cols]
        grp = [lf[SUBLANES * r:SUBLANES * (r + 1)] for r in range(n_grp)]
        for sh in (1, 2, 4):
            grp = [p + jnp.where(sub >= sh, pltpu.roll(p, sh, axis=0), 0.0) for p in grp]
        g_grp, ends = [], []
        for r in range(n_grp):
            gp = grp[r] if r == 0 else grp[r] + ends[r - 1]
            g_grp.append(gp)
            ends.append(bcast_row(gp, SUBLANES - 1))
        q_dec = (q * jnp.exp2(jnp.concatenate(g_grp, axis=0))).astype(BF16)
        diag = _dot_nt(q_bf, k_bf) * mask_ref[0]
        return dict(q=q, k=k, g_grp=g_grp, ends=ends, q_dec=q_dec, scores=diag)

    def level(hd, m, lvl):
        g_grp, ends = hd["g_grp"], hd["ends"]
        if m >= SUBLANES:
            gm = m // SUBLANES
            refs = [ends[(r // (2 * gm)) * 2 * gm + gm - 1] for r in range(n_grp)]
            args = [gp - rf if (SUBLANES * r) & m else rf - gp
                    for r, (gp, rf) in enumerate(zip(g_grp, refs))]
        else:
            if m == 4:
                refs = [bcast_row(gp, 3) for gp in g_grp]
            elif m == 2:
                refs = [jnp.where(sub < 4, bcast_row(gp, 1), bcast_row(gp, 5)) for gp in g_grp]
            else:
                refs = [jnp.where((sub & 1) != 0, pltpu.roll(gp, 1, axis=0), gp) for gp in g_grp]
            args = [(gp - rf) * sign_small[m] for gp, rf in zip(g_grp, refs)]
        if m >= SUBLANES:
            zero = jnp.zeros((SUBLANES, HG_DK), F32)
            q_rows, k_rows = [], []
            for r, a in enumerate(args):
                rows_r = slice(SUBLANES * r, SUBLANES * (r + 1))
                e_r = jnp.exp2(a)
                is_query = bool((SUBLANES * r) & m)
                q_rows.append(hd["q"][rows_r] * e_r if is_query else zero)
                k_rows.append(zero if is_query else hd["k"][rows_r] * e_r)
            q_side = jnp.concatenate(q_rows, axis=0)
            k_side = jnp.concatenate(k_rows, axis=0)
        else:
            e = jnp.exp2(jnp.concatenate(args, axis=0))
            q_side = hd["q"] * e
            k_side = hd["k"] * e
        part = _dot_nt(q_side.astype(BF16), k_side.astype(BF16))
        if 2 * m == HG_CHUNK:
            hd["scores"] = hd["scores"] + part
        else:
            hd["scores"] = hd["scores"] + part * mask_ref[lvl]

    def finish(rows, hh, hd):
        cols = slice(hh * HG_DK, (hh + 1) * HG_DK)
        v = v_ref[rows, cols]
        state = state_ref[hh]
        o = _dot_nt(hd["q_dec"], state.astype(BF16)) + _dot(hd["scores"].astype(BF16), v)
        g_end = hd["ends"][n_grp - 1]
        to_end = jnp.exp2(jnp.concatenate([g_end - gp for gp in hd["g_grp"]], axis=0))
        k_dec = (hd["k"] * to_end).astype(BF16)
        state_ref[hh] = jnp.exp2(g_end[0:1, :]) * state + _dot_tn(v, k_dec)
        y = _rms_norm(o, gn_ref[:, cols]) * og_ref[rows, cols].astype(F32)
        o_ref[rows, cols] = y.astype(o_ref.dtype)

    def chunk_group(c, carry):
        rows = [pl.ds(pl.multiple_of((HG_CHUNKS_PER_ITER * c + i) * HG_CHUNK, HG_CHUNK), HG_CHUNK)
                for i in range(HG_CHUNKS_PER_ITER)]
        work = [(r, hh, prepare(r, hh)) for r in rows for hh in range(HEADS)]
        m, lvl = 1, 1
        while m < HG_CHUNK:
            for _, _, hd in work:
                level(hd, m, lvl)
            m *= 2
            lvl += 1
        for r, hh, hd in work:
            finish(r, hh, hd)
        return carry

    assert n_chunks % HG_CHUNKS_PER_ITER == 0, n_chunks
    lax.fori_loop(0, n_chunks // HG_CHUNKS_PER_ITER, chunk_group, 0)


def _hgrn_call(hq, lf, hk, hv, og, gnorm, *, layer, batch, seq, tb):
    tokens = hq.shape[0]
    per_b = seq // tb
    width = HEADS * HG_DK
    blk = pl.BlockSpec((tb, width), lambda b, i: (b * per_b + i, 0))
    masks = jnp.asarray(_hgrn_masks())
    return pl.pallas_call(
        functools.partial(_hgrn_kernel, n_chunks=tb // HG_CHUNK),
        grid=(batch, per_b),
        in_specs=[blk, blk, blk, blk, blk, _stacked(gnorm, layer), _resident(masks.shape)],
        out_specs=blk,
        out_shape=jax.ShapeDtypeStruct((tokens, width), BF16),
        scratch_shapes=[pltpu.VMEM((HEADS, HG_DV, HG_DK), F32)],
        compiler_params=_params("arbitrary", "arbitrary"),
        name="hgrn",
    )(hq, lf, hk, hv, og, gnorm, masks)


def _attn_kernel(q_ref, k_ref, vt_ref, o_ref, m_ref, acc_ref, s_ref, bmax_ref, *, tq, hps):
    qi = pl.program_id(2)
    m_ref[...] = jnp.full_like(m_ref, -jnp.inf)
    acc_ref[...] = jnp.zeros_like(acc_ref)
    ones = jnp.ones((ONES_ROWS, tq), BF16)

    def scores(a, j, slot):
        start = pl.multiple_of(j * tq, tq)
        s_t = _dot_nt(k_ref[a, pl.ds(start, tq), :], q_ref[a])
        s_ref[a, slot] = s_t
        bmax_ref[a, slot] = jnp.max(s_t, axis=0, keepdims=True)

    def accumulate(a, j, slot, causal):
        start = pl.multiple_of(j * tq, tq)
        s_t = s_ref[a, slot]
        if causal:
            kpos = lax.broadcasted_iota(jnp.int32, (tq, tq), 0)
            qpos = lax.broadcasted_iota(jnp.int32, (tq, tq), 1)
            s_t = jnp.where(kpos <= qpos, s_t, MASK_VALUE)
            block_max = jnp.max(s_t, axis=0, keepdims=True)
        else:
            block_max = bmax_ref[a, slot]
        m_old = m_ref[a]
        m_new = jnp.maximum(m_old, block_max)
        p = jnp.exp2((s_t - m_new).astype(BF16))
        alpha = jnp.exp2(m_old - m_new)
        v_ext = jnp.concatenate([vt_ref[a * MLA_V:(a + 1) * MLA_V, pl.ds(start, tq)], ones], axis=0)
        acc_ref[a] = alpha * acc_ref[a] + _dot(v_ext, p)
        m_ref[a] = m_new

    for a in range(hps):
        scores(a, 0, 0)

    def step(j, slot):
        for a in range(hps):
            scores(a, j + 1, 1 - slot)
            accumulate(a, j, slot, False)

    def body(jj, carry):
        step(2 * jj, 0)
        step(2 * jj + 1, 1)
        return carry

    lax.fori_loop(0, qi // 2, body, 0)

    @pl.when(qi % 2 == 1)
    def _():
        step(qi - 1, 0)
        for a in range(hps):
            accumulate(a, qi, 1, True)

    @pl.when(qi % 2 == 0)
    def _():
        for a in range(hps):
            accumulate(a, qi, 0, True)

    for a in range(hps):
        acc = acc_ref[a]
        o_t = acc[0:MLA_V] * (1.0 / acc[MLA_V:MLA_V + 1])
        o_ref[:, a * MLA_V:(a + 1) * MLA_V] = o_t.T.astype(o_ref.dtype)


def _attn_call(q, k, vt, *, batch, seq, tq, hps):
    tokens = vt.shape[1]
    nq = seq // tq
    return pl.pallas_call(
        functools.partial(_attn_kernel, tq=tq, hps=hps),
        grid=(batch, HEADS // hps, nq),
        in_specs=[
            pl.BlockSpec((hps, tq, MLA_QK), lambda b, g, i: (g, b * nq + i, 0)),
            pl.BlockSpec((hps, seq, MLA_QK), lambda b, g, i: (g, b, 0),
                         pipeline_mode=pl.Buffered(1)),
            pl.BlockSpec((hps * MLA_V, seq), lambda b, g, i: (g, b),
                         pipeline_mode=pl.Buffered(1)),
        ],
        out_specs=pl.BlockSpec((tq, hps * MLA_V), lambda b, g, i: (b * nq + i, g)),
        out_shape=jax.ShapeDtypeStruct((tokens, HEADS * MLA_V), BF16),
        scratch_shapes=[pltpu.VMEM((hps, 1, tq), F32),
                        pltpu.VMEM((hps, MLA_V + ONES_ROWS, tq), F32),
                        pltpu.VMEM((hps, 2, tq, tq), F32),
                        pltpu.VMEM((hps, 2, 1, tq), F32)],
        compiler_params=_params("arbitrary", "arbitrary", "arbitrary"),
        name="mla_attn",
    )(q, k, vt)


def _merge_kernel(h_ref, ada_ref, yh_ref, om_ref, sga_ref, sgb_ref, wbh_ref, wbm_ref, wo_ref,
                  lng_ref, lnb_ref, o_ref):
    gate = ada_ref[0, 5:6, :]
    parts = _row_parts(h_ref.shape[0], 2)
    y_hg = [_dot(yh_ref[r, :], wbh_ref[...]) for r in parts]
    y_mla = [_dot(om_ref[r, :], wbm_ref[...]) for r in parts]
    merged = [(sga_ref[r, :].astype(F32) * a + sgb_ref[r, :].astype(F32) * b).astype(BF16)
              for r, a, b in zip(parts, y_hg, y_mla)]
    y = [_dot(m, wo_ref[...]) for m in merged]
    for r, yy in zip(parts, y):
        res = DN_ALPHA * h_ref[r, :] + (1.0 + gate) * yy
        o_ref[r, :] = _layer_norm(res, lng_ref[...], lnb_ref[...])


def _merge_call(h, ada, yh, om, sga, sgb, wbh, wbm, wo, ln_g, ln_b, *, layer, seq, tm):
    tokens = h.shape[0]
    row = pl.BlockSpec((tm, D_MODEL), lambda i: (i, 0))
    return pl.pallas_call(
        _merge_kernel,
        grid=(tokens // tm,),
        in_specs=[row, _ada_spec(layer, seq // tm), row, row, row, row,
                  _stacked(wbh, layer), _stacked(wbm, layer), _stacked(wo, layer),
                  _stacked(ln_g, 3 * layer + 1), _stacked(ln_b, 3 * layer + 1)],
        out_specs=row,
        out_shape=jax.ShapeDtypeStruct((tokens, D_MODEL), F32),
        compiler_params=_params("arbitrary"),
        name="merge_out",
    )(h, ada, yh, om, sga, sgb, wbh, wbm, wo, ln_g, ln_b)


def _mixer_weights(w_in, w_uq, w_ukv, q_norm_g, kv_norm_g):
    n = HEADS * HG_DK
    c_cq = 4 * n
    c_ckv = c_cq + Q_LORA
    c_kr = c_ckv + KV_LORA
    c_ga = c_kr + MLA_ROPE
    half = MLA_ROPE // 2
    w_in = w_in.astype(BF16)
    kr = w_in[:, :, c_kr:c_ga]
    kr_swapped = jnp.concatenate([kr[:, :, half:], kr[:, :, :half]], axis=2)
    pad = jnp.zeros((DEPTH, D_MODEL, 128 - MLA_ROPE), BF16)
    w_lat = jnp.concatenate([w_in[:, :, c_cq:c_kr], kr, pad, kr_swapped, pad], axis=2)

    uq = w_uq.astype(BF16).reshape(DEPTH, Q_LORA, HEADS, MLA_QK)
    uq_nope = uq[..., :MLA_NOPE].reshape(DEPTH, Q_LORA, HEADS * MLA_NOPE)
    uq_rope = uq[..., MLA_NOPE:]
    uq_rope_sw = jnp.concatenate([uq_rope[..., half:], uq_rope[..., :half]], axis=3)
    w_uq_p = jnp.concatenate([uq_nope, uq_rope.reshape(DEPTH, Q_LORA, HEADS * MLA_ROPE),
                              uq_rope_sw.reshape(DEPTH, Q_LORA, HEADS * MLA_ROPE)], axis=2)

    ukv = w_ukv.astype(BF16).reshape(DEPTH, KV_LORA, HEADS, MLA_NOPE + MLA_V)
    w_uk = ukv[..., :MLA_NOPE].reshape(DEPTH, KV_LORA, HEADS * MLA_NOPE)
    w_vt = jnp.swapaxes(ukv[..., MLA_NOPE:].reshape(DEPTH, KV_LORA, HEADS * MLA_V), 1, 2)
    return {
        "w_hg": w_in[:, :, :c_cq],
        "w_lat": w_lat,
        "w_gate": w_in[:, :, c_ga:],
        "w_uq": w_uq_p,
        "w_uk": w_uk,
        "w_vt": w_vt,
        "q_norm_g": q_norm_g.reshape(DEPTH, 1, Q_LORA),
        "kv_norm_g": kv_norm_g.reshape(DEPTH, 1, KV_LORA),
    }


def _tile(seq, want):
    t = min(want, seq)
    assert seq % t == 0, (seq, t)
    return t


def kernel(x, c, positions, ada_w, ada_b, ln_g, ln_b, ffn1_gate, ffn1_up, ffn1_down, w_in,
           hg_lower_bound, hg_norm_g, mla_q_norm_g, mla_w_uq, mla_kv_norm_g, mla_w_ukv,
           w_branch_hg, w_branch_mla, w_out, ffn2_gate, ffn2_up, ffn2_down):
    batch, seq, _ = x.shape
    tokens = batch * seq
    tm_ffn = _tile(seq, 256 * FFN_ROW_GROUPS)
    tm_in = _tile(seq, 512)
    tm_merge = _tile(seq, 512)
    tb_hgrn = _tile(seq, 512)
    tq = _tile(seq, 512)
    attn_heads_per_step = 4

    ada = _ada_call(c, ada_w, ada_b).reshape(DEPTH, batch, N_ADA, D_MODEL)
    cos_t, sin_t = _rope_call(positions)
    lbraw = hg_lower_bound.astype(F32)

    ln_g3 = ln_g.reshape(DEPTH * 3, 1, D_MODEL)
    ln_b3 = ln_b.reshape(DEPTH * 3, 1, D_MODEL)
    ffn1 = [w.astype(BF16) for w in (ffn1_gate, ffn1_up, ffn1_down)]
    ffn2 = [w.astype(BF16) for w in (ffn2_gate, ffn2_up, ffn2_down)]
    w_mix = _mixer_weights(w_in, mla_w_uq, mla_w_ukv, mla_q_norm_g, mla_kv_norm_g)
    w_tail = [w.astype(BF16) for w in (w_branch_hg, w_branch_mla, w_out)]
    gnorm = hg_norm_g.reshape(DEPTH, 1, HEADS * HG_DV)

    h = x.reshape(tokens, D_MODEL)
    for l in range(DEPTH):
        h = _ffn_call(h, ada, *ffn1, ln_g3, ln_b3, layer=l, sub=0, seq=seq, tm=tm_ffn)
        hq, lf, hk, hv, og, q, k, vt, sga, sgb = _inproj_call(
            h, ada, w_mix, lbraw, cos_t, sin_t, layer=l, seq=seq, tm=tm_in)
        yh = _hgrn_call(hq, lf, hk, hv, og, gnorm, layer=l, batch=batch, seq=seq, tb=tb_hgrn)
        om = _attn_call(q, k, vt, batch=batch, seq=seq, tq=tq, hps=attn_heads_per_step)
        h = _merge_call(h, ada, yh, om, sga, sgb, *w_tail, ln_g3, ln_b3,
                        layer=l, seq=seq, tm=tm_merge)
        h = _ffn_call(h, ada, *ffn2, ln_g3, ln_b3, layer=l, sub=2, seq=seq, tm=tm_ffn)
    return h.reshape(batch, seq, D_MODEL)
```

```python
import functools

import jax
import jax.numpy as jnp
import numpy as np
from jax import lax
from jax.experimental import pallas as pl
from jax.experimental.pallas import tpu as pltpu

F32 = jnp.float32
BF16 = jnp.bfloat16

D_MODEL = 1024
DEPTH = 4
HEADS = 8
HG_DK = 128
HG_DV = 128
HG_CHUNK = 64
LB_FLOOR = 1e-30
MLA_NOPE = 128
MLA_ROPE = 64
MLA_QK = MLA_NOPE + MLA_ROPE
MLA_V = 128
Q_LORA = 384
KV_LORA = 256
ROPE_THETA = 10000.0
MASK_VALUE = -1e30
D_FF = 2816
DN_ALPHA = (2.0 * DEPTH) ** 0.25
LN_EPS = 1e-5
RMS_EPS = 1e-6
N_ADA = 9
LOG2E = 1.4426950408889634
ATTN_SCALE_LOG2E = float(MLA_QK) ** -0.5 * LOG2E
SUBLANES = 8
ONES_ROWS = 16
FFN_ROW_GROUPS = 4
HG_CHUNKS_PER_ITER = 8

VMEM_LIMIT_BYTES = 56 * 1024 * 1024
ATTN_VMEM_LIMIT_BYTES = 63 * 1024 * 1024


def _params(*sem, vmem_limit_bytes=VMEM_LIMIT_BYTES):
    return pltpu.CompilerParams(dimension_semantics=sem, vmem_limit_bytes=vmem_limit_bytes)


def _resident(shape):
    zeros = (0,) * len(shape)
    return pl.BlockSpec(shape, lambda *_: zeros, pipeline_mode=pl.Buffered(1))


def _stacked(arr, index):
    rest = (0,) * (arr.ndim - 1)
    return pl.BlockSpec((None,) + arr.shape[1:], lambda *_: (index,) + rest,
                        pipeline_mode=pl.Buffered(1))


def _silu(x):
    return x * jax.nn.sigmoid(x)


def _dot(a, b):
    return jnp.dot(a, b, preferred_element_type=F32)


def _dot_nt(a, b):
    return lax.dot_general(a, b, (((1,), (1,)), ((), ())), preferred_element_type=F32)


def _dot_tn(a, b):
    return lax.dot_general(a, b, (((0,), (0,)), ((), ())), preferred_element_type=F32)


def _row_parts(rows, n):
    assert rows % n == 0, (rows, n)
    return [slice(i * rows // n, (i + 1) * rows // n) for i in range(n)]


def _layer_norm(r, g, b):
    mu = jnp.mean(r, axis=-1, keepdims=True)
    d = r - mu
    var = jnp.mean(d * d, axis=-1, keepdims=True)
    return d * lax.rsqrt(var + LN_EPS) * g + b


def _rms_norm(x, g):
    return x * lax.rsqrt(jnp.mean(x * x, axis=-1, keepdims=True) + RMS_EPS) * g


def _ada_kernel(c_ref, w_ref, b_ref, o_ref):
    cond = _silu(c_ref[...])
    o_ref[0] = jnp.dot(cond, w_ref[0], preferred_element_type=F32,
                       precision=lax.Precision.HIGHEST) + b_ref[0]


def _ada_call(c, ada_w, ada_b):
    batch = c.shape[0]
    n_out = ada_w.shape[-1]
    tn = D_MODEL
    return pl.pallas_call(
        _ada_kernel,
        grid=(DEPTH, n_out // tn),
        in_specs=[
            pl.BlockSpec((batch, D_MODEL), lambda l, j: (0, 0)),
            pl.BlockSpec((1, D_MODEL, tn), lambda l, j: (l, 0, j)),
            pl.BlockSpec((1, 1, tn), lambda l, j: (l, 0, j)),
        ],
        out_specs=pl.BlockSpec((1, batch, tn), lambda l, j: (l, 0, j)),
        out_shape=jax.ShapeDtypeStruct((DEPTH, batch, n_out), F32),
        compiler_params=_params("arbitrary", "arbitrary"),
        name="ada",
    )(c, ada_w, ada_b.reshape(DEPTH, 1, n_out))


def _rope_kernel(pos_ref, inv_ref, sign_ref, cos_ref, sin_ref):
    ang = pos_ref[...].astype(F32) * inv_ref[...]
    cos_ref[...] = jnp.cos(ang)
    sin_ref[...] = jnp.sin(ang) * sign_ref[...]


def _rope_call(positions):
    tokens = positions.size
    tm = min(1024, tokens)
    half = MLA_ROPE // 2
    inv = 1.0 / (ROPE_THETA ** (jnp.arange(0, MLA_ROPE, 2, dtype=F32) / MLA_ROPE))
    inv_t = jnp.tile(inv, 4).reshape(1, 4 * half)
    sign = jnp.tile(jnp.concatenate([-jnp.ones((half,), F32), jnp.ones((half,), F32)]), 2)
    sign = sign.reshape(1, 4 * half)
    row = pl.BlockSpec((tm, 4 * half), lambda i: (i, 0))
    const = pl.BlockSpec((1, 4 * half), lambda i: (0, 0))
    return pl.pallas_call(
        _rope_kernel,
        grid=(tokens // tm,),
        in_specs=[pl.BlockSpec((tm, 1), lambda i: (i, 0)), const, const],
        out_specs=[row, row],
        out_shape=[jax.ShapeDtypeStruct((tokens, 4 * half), F32)] * 2,
        compiler_params=_params("arbitrary"),
        name="rope_tables",
    )(positions.reshape(tokens, 1), inv_t, sign)


def _ffn_kernel(h_ref, ada_ref, wg_ref, wu_ref, wd_ref, lng_ref, lnb_ref, o_ref, *, k0):
    shift = ada_ref[0, k0:k0 + 1, :]
    scale = ada_ref[0, k0 + 1:k0 + 2, :]
    gate = ada_ref[0, k0 + 2:k0 + 3, :]
    parts = _row_parts(h_ref.shape[0], FFN_ROW_GROUPS)
    u = [(h_ref[r, :] * (1.0 + scale) + shift).astype(BF16) for r in parts]
    act = []
    for uu in u:
        g = _dot(uu, wg_ref[...])
        up = _dot(uu, wu_ref[...])
        act.append((_silu(g) * up).astype(BF16))
    y = [_dot(a, wd_ref[...]) for a in act]
    for r, yy in zip(parts, y):
        res = DN_ALPHA * h_ref[r, :] + (0.5 * (1.0 + gate)) * yy
        o_ref[r, :] = _layer_norm(res, lng_ref[...], lnb_ref[...])


def _ada_spec(layer, per_b):
    return pl.BlockSpec((None, 1, N_ADA, D_MODEL), lambda i: (layer, i // per_b, 0, 0))


def _ffn_call(h, ada, wg, wu, wd, ln_g, ln_b, *, layer, sub, seq, tm):
    tokens = h.shape[0]
    row = pl.BlockSpec((tm, D_MODEL), lambda i: (i, 0))
    return pl.pallas_call(
        functools.partial(_ffn_kernel, k0=3 * sub),
        grid=(tokens // tm,),
        in_specs=[
            row,
            _ada_spec(layer, seq // tm),
            _stacked(wg, layer),
            _stacked(wu, layer),
            _stacked(wd, layer),
            _stacked(ln_g, 3 * layer + sub),
            _stacked(ln_b, 3 * layer + sub),
        ],
        out_specs=row,
        out_shape=jax.ShapeDtypeStruct((tokens, D_MODEL), F32),
        compiler_params=_params("arbitrary"),
        name="ffn",
    )(h, ada, wg, wu, wd, ln_g, ln_b)


def _inproj_kernel(h_ref, ada_ref, whg_ref, wlat_ref, wgate_ref, lbraw_ref, qng_ref, wuq_ref,
                   kvng_ref, wuk_ref, wvt_ref, cos_ref, sin_ref,
                   hq_ref, lf_ref, hk_ref, hv_ref, og_ref, q_ref, k_ref, vt_ref, sga_ref, sgb_ref,
                   *, layer):
    shift = ada_ref[0, 3:4, :]
    scale = ada_ref[0, 4:5, :]

    raw = lbraw_ref[...]
    e = jnp.exp(raw - jnp.max(raw, axis=0, keepdims=True))
    lb = jnp.zeros((1, HEADS * HG_DK), F32)
    for i in range(1, layer + 1):
        lb = lb + e[i:i + 1, :]
    lb = lb / jnp.sum(e, axis=0, keepdims=True)
    one_minus_lb = 1.0 - lb

    def project(rows):
        u = (h_ref[rows, :] * (1.0 + scale) + shift).astype(BF16)
        n = HEADS * HG_DK
        zf = _dot(u, whg_ref[:, n:2 * n])
        t = jnp.exp(-jnp.abs(zf))
        r = 1.0 / (1.0 + t)
        tr = t * r
        pos = zf >= 0.0
        sig_pos = jnp.where(pos, r, tr)
        sig_neg = jnp.where(pos, tr, r)
        lf_ref[rows, :] = jnp.log2(jnp.maximum(lb, LB_FLOOR) + one_minus_lb * sig_pos)
        hk_ref[rows, :] = (one_minus_lb * sig_neg).astype(BF16)

        zgate = _dot(u, wgate_ref[...])
        sga_ref[rows, :] = jax.nn.sigmoid(zgate[:, 0:D_MODEL]).astype(sga_ref.dtype)
        sgb_ref[rows, :] = jax.nn.sigmoid(zgate[:, D_MODEL:2 * D_MODEL]).astype(sgb_ref.dtype)

        zl = _dot(u, wlat_ref[...])
        cos_t = cos_ref[rows, :]
        sin_t = sin_ref[rows, :]
        cqn = _rms_norm(zl[:, 0:Q_LORA], qng_ref[...]).astype(BF16)
        qf = _dot(cqn, wuq_ref[...]) * ATTN_SCALE_LOG2E
        nq = HEADS * MLA_NOPE
        nr = HEADS * MLA_ROPE
        cos_q = jnp.concatenate([cos_t] * (nr // 128), axis=1)
        sin_q = jnp.concatenate([sin_t] * (nr // 128), axis=1)
        q_rope = qf[:, nq:nq + nr] * cos_q + qf[:, nq + nr:nq + 2 * nr] * sin_q
        ckvn = _rms_norm(zl[:, Q_LORA:Q_LORA + KV_LORA], kvng_ref[...]).astype(BF16)
        k_nope = _dot(ckvn, wuk_ref[...])
        vt_ref[:, rows] = _dot_nt(wvt_ref[...], ckvn).astype(BF16)
        c0 = Q_LORA + KV_LORA
        k_rope = (zl[:, c0:c0 + 128] * cos_t + zl[:, c0 + 128:c0 + 256] * sin_t)[:, 0:MLA_ROPE]
        k_rope = k_rope.astype(BF16)
        for hh in range(HEADS):
            q_ref[hh, rows, 0:MLA_NOPE] = qf[:, hh * MLA_NOPE:(hh + 1) * MLA_NOPE].astype(BF16)
            q_ref[hh, rows, MLA_NOPE:MLA_QK] = (
                q_rope[:, hh * MLA_ROPE:(hh + 1) * MLA_ROPE].astype(BF16))
            k_ref[hh, rows, 0:MLA_NOPE] = k_nope[:, hh * MLA_NOPE:(hh + 1) * MLA_NOPE].astype(BF16)
            k_ref[hh, rows, MLA_NOPE:MLA_QK] = k_rope

        hq_ref[rows, :] = _silu(_dot(u, whg_ref[:, 0:n])).astype(BF16)
        og_ref[rows, :] = _silu(_dot(u, whg_ref[:, 3 * n:4 * n])).astype(og_ref.dtype)
        hv_ref[rows, :] = _dot(u, whg_ref[:, 2 * n:3 * n]).astype(BF16)

    for rows in _row_parts(h_ref.shape[0], 2):
        project(rows)


def _inproj_call(h, ada, w, lbraw, cos_t, sin_t, *, layer, seq, tm):
    tokens = h.shape[0]
    row = pl.BlockSpec((tm, D_MODEL), lambda i: (i, 0))
    tab = pl.BlockSpec((tm, 128), lambda i: (i, 0))
    headed = pl.BlockSpec((HEADS, tm, MLA_QK), lambda i: (0, i, 0))
    wide = jax.ShapeDtypeStruct((tokens, D_MODEL), BF16)
    qk = jax.ShapeDtypeStruct((HEADS, tokens, MLA_QK), BF16)
    return pl.pallas_call(
        functools.partial(_inproj_kernel, layer=layer),
        grid=(tokens // tm,),
        in_specs=[
            row,
            _ada_spec(layer, seq // tm),
            _stacked(w["w_hg"], layer),
            _stacked(w["w_lat"], layer),
            _stacked(w["w_gate"], layer),
            _resident(lbraw.shape),
            _stacked(w["q_norm_g"], layer),
            _stacked(w["w_uq"], layer),
            _stacked(w["kv_norm_g"], layer),
            _stacked(w["w_uk"], layer),
            _stacked(w["w_vt"], layer),
            tab, tab,
        ],
        out_specs=[row, row, row, row, row, headed, headed,
                   pl.BlockSpec((HEADS * MLA_V, tm), lambda i: (0, i)), row, row],
        out_shape=[wide, jax.ShapeDtypeStruct((tokens, D_MODEL), F32), wide, wide, wide,
                   qk, qk, jax.ShapeDtypeStruct((HEADS * MLA_V, tokens), BF16), wide, wide],
        compiler_params=_params("arbitrary"),
        name="inproj",
    )(h, ada, w["w_hg"], w["w_lat"], w["w_gate"], lbraw, w["q_norm_g"], w["w_uq"],
      w["kv_norm_g"], w["w_uk"], w["w_vt"], cos_t, sin_t)


def _hgrn_masks():
    t = np.arange(HG_CHUNK)[:, None]
    s = np.arange(HG_CHUNK)[None, :]
    masks = [(t == s)]
    m = 1
    while m < HG_CHUNK:
        masks.append((t // (2 * m) == s // (2 * m)) & ((t & m) != 0) & ((s & m) == 0))
        m *= 2
    return np.stack(masks).astype(np.float32)


def _hgrn_kernel(q_ref, lf_ref, k_ref, v_ref, og_ref, gn_ref, mask_ref, o_ref, state_ref, *, n_chunks):
    @pl.when(pl.program_id(1) == 0)
    def _():
        state_ref[...] = jnp.zeros_like(state_ref)

    n_grp = HG_CHUNK // SUBLANES
    sub = lax.broadcasted_iota(jnp.int32, (SUBLANES, HG_DK), 0)
    sign_small = {m: jnp.where((sub & m) != 0, 1.0, -1.0) for m in (1, 2, 4)}

    def bcast_row(x, i):
        return jnp.broadcast_to(x[i:i + 1, :], (SUBLANES, HG_DK))

    def prepare(rows, hh):
        cols = slice(hh * HG_DK, (hh + 1) * HG_DK)
        q_bf = q_ref[rows, cols]
        k_bf = k_ref[rows, cols]
        q = q_bf.astype(F32)
        k = k_bf.astype(F32)

        lf = lf_ref[rows, cols]
        grp = [lf[SUBLANES * r:SUBLANES * (r + 1)] for r in range(n_grp)]
        for sh in (1, 2, 4):
            grp = [p + jnp.where(sub >= sh, pltpu.roll(p, sh, axis=0), 0.0) for p in grp]
        g_grp, ends = [], []
        for r in range(n_grp):
            gp = grp[r] if r == 0 else grp[r] + ends[r - 1]
            g_grp.append(gp)
            ends.append(bcast_row(gp, SUBLANES - 1))
        q_dec = (q * jnp.exp2(jnp.concatenate(g_grp, axis=0))).astype(BF16)
        diag = _dot_nt(q_bf, k_bf) * mask_ref[0]
        return dict(q=q, k=k, g_grp=g_grp, ends=ends, q_dec=q_dec, scores=diag)

    def level(hd, m, lvl):
        g_grp, ends = hd["g_grp"], hd["ends"]
        if m >= SUBLANES:
            gm = m // SUBLANES
            refs = [ends[(r // (2 * gm)) * 2 * gm + gm - 1] for r in range(n_grp)]
            args = [gp - rf if (SUBLANES * r) & m else rf - gp
                    for r, (gp, rf) in enumerate(zip(g_grp, refs))]
        else:
            if m == 4:
                refs = [bcast_row(gp, 3) for gp in g_grp]
            elif m == 2:
                refs = [jnp.where(sub < 4, bcast_row(gp, 1), bcast_row(gp, 5)) for gp in g_grp]
            else:
                refs = [jnp.where((sub & 1) != 0, pltpu.roll(gp, 1, axis=0), gp) for gp in g_grp]
            args = [(gp - rf) * sign_small[m] for gp, rf in zip(g_grp, refs)]
        if m >= SUBLANES:
            zero = jnp.zeros((SUBLANES, HG_DK), F32)
            q_rows, k_rows = [], []
            for r, a in enumerate(args):
                rows_r = slice(SUBLANES * r, SUBLANES * (r + 1))
                e_r = jnp.exp2(a)
                is_query = bool((SUBLANES * r) & m)
                q_rows.append(hd["q"][rows_r] * e_r if is_query else zero)
                k_rows.append(zero if is_query else hd["k"][rows_r] * e_r)
            q_side = jnp.concatenate(q_rows, axis=0)
            k_side = jnp.concatenate(k_rows, axis=0)
        else:
            e = jnp.exp2(jnp.concatenate(args, axis=0))
            q_side = hd["q"] * e
            k_side = hd["k"] * e
        part = _dot_nt(q_side.astype(BF16), k_side.astype(BF16))
        if 2 * m == HG_CHUNK:
            hd["scores"] = hd["scores"] + part
        else:
            hd["scores"] = hd["scores"] + part * mask_ref[lvl]

    def finish(rows, hh, hd):
        cols = slice(hh * HG_DK, (hh + 1) * HG_DK)
        v = v_ref[rows, cols]
        state = state_ref[hh]
        o = _dot_nt(hd["q_dec"], state.astype(BF16)) + _dot(hd["scores"].astype(BF16), v)
        g_end = hd["ends"][n_grp - 1]
        to_end = jnp.exp2(jnp.concatenate([g_end - gp for gp in hd["g_grp"]], axis=0))
        k_dec = (hd["k"] * to_end).astype(BF16)
        state_ref[hh] = jnp.exp2(g_end[0:1, :]) * state + _dot_tn(v, k_dec)
        y = _rms_norm(o, gn_ref[:, cols]) * og_ref[rows, cols].astype(F32)
        o_ref[rows, cols] = y.astype(o_ref.dtype)

    def chunk_group(c, carry):
        rows = [pl.ds(pl.multiple_of((HG_CHUNKS_PER_ITER * c + i) * HG_CHUNK, HG_CHUNK), HG_CHUNK)
                for i in range(HG_CHUNKS_PER_ITER)]
        work = [(r, hh, prepare(r, hh)) for r in rows for hh in range(HEADS)]
        m, lvl = 1, 1
        while m < HG_CHUNK:
            for _, _, hd in work:
                level(hd, m, lvl)
            m *= 2
            lvl += 1
        for r, hh, hd in work:
            finish(r, hh, hd)
        return carry

    assert n_chunks % HG_CHUNKS_PER_ITER == 0, n_chunks
    lax.fori_loop(0, n_chunks // HG_CHUNKS_PER_ITER, chunk_group, 0)


def _hgrn_call(hq, lf, hk, hv, og, gnorm, *, layer, batch, seq, tb):
    tokens = hq.shape[0]
    per_b = seq // tb
    width = HEADS * HG_DK
    blk = pl.BlockSpec((tb, width), lambda b, i: (b * per_b + i, 0))
    masks = jnp.asarray(_hgrn_masks())
    return pl.pallas_call(
        functools.partial(_hgrn_kernel, n_chunks=tb // HG_CHUNK),
        grid=(batch, per_b),
        in_specs=[blk, blk, blk, blk, blk, _stacked(gnorm, layer), _resident(masks.shape)],
        out_specs=blk,
        out_shape=jax.ShapeDtypeStruct((tokens, width), BF16),
        scratch_shapes=[pltpu.VMEM((HEADS, HG_DV, HG_DK), F32)],
        compiler_params=_params("arbitrary", "arbitrary"),
        name="hgrn",
    )(hq, lf, hk, hv, og, gnorm, masks)


def _attn_kernel(q_ref, k_ref, vt_ref, o_ref, m_ref, acc_ref, s_ref, bmax_ref, *, tq, hps):
    qi = pl.program_id(2)
    m_ref[...] = jnp.full_like(m_ref, -jnp.inf)
    acc_ref[...] = jnp.zeros_like(acc_ref)
    ones = jnp.ones((ONES_ROWS, tq), BF16)

    def scores(a, j, slot):
        start = pl.multiple_of(j * tq, tq)
        s_t = _dot_nt(k_ref[a, pl.ds(start, tq), :], q_ref[a])
        s_ref[a, slot] = s_t
        bmax_ref[a, slot] = jnp.max(s_t, axis=0, keepdims=True)

    def accumulate(a, j, slot, causal):
        start = pl.multiple_of(j * tq, tq)
        s_t = s_ref[a, slot]
        if causal:
            kpos = lax.broadcasted_iota(jnp.int32, (tq, tq), 0)
            qpos = lax.broadcasted_iota(jnp.int32, (tq, tq), 1)
            s_t = jnp.where(kpos <= qpos, s_t, MASK_VALUE)
            block_max = jnp.max(s_t, axis=0, keepdims=True)
        else:
            block_max = bmax_ref[a, slot]
        m_old = m_ref[a]
        m_new = jnp.maximum(m_old, block_max)
        p = jnp.exp2((s_t - m_new).astype(BF16))
        alpha = jnp.exp2(m_old - m_new)
        v_ext = jnp.concatenate([vt_ref[a * MLA_V:(a + 1) * MLA_V, pl.ds(start, tq)], ones], axis=0)
        acc_ref[a] = alpha * acc_ref[a] + _dot(v_ext, p)
        m_ref[a] = m_new

    for a in range(hps):
        scores(a, 0, 0)

    def step(j, slot):
        for a in range(hps):
            scores(a, j + 1, 1 - slot)
            accumulate(a, j, slot, False)

    def body(jj, carry):
        step(2 * jj, 0)
        step(2 * jj + 1, 1)
        return carry

    lax.fori_loop(0, qi // 2, body, 0)

    @pl.when(qi % 2 == 1)
    def _():
        step(qi - 1, 0)
        for a in range(hps):
            accumulate(a, qi, 1, True)

    @pl.when(qi % 2 == 0)
    def _():
        for a in range(hps):
            accumulate(a, qi, 0, True)

    for a in range(hps):
        acc = acc_ref[a]
        o_t = acc[0:MLA_V] * (1.0 / acc[MLA_V:MLA_V + 1])
        o_ref[:, a * MLA_V:(a + 1) * MLA_V] = o_t.T.astype(o_ref.dtype)


def _attn_call(q, k, vt, *, batch, seq, tq, hps):
    tokens = vt.shape[1]
    nq = seq // tq
    return pl.pallas_call(
        functools.partial(_attn_kernel, tq=tq, hps=hps),
        grid=(batch, HEADS // hps, nq),
        in_specs=[
            pl.BlockSpec((hps, tq, MLA_QK), lambda b, g, i: (g, b * nq + i, 0)),
            pl.BlockSpec((hps, seq, MLA_QK), lambda b, g, i: (g, b, 0)),
            pl.BlockSpec((hps * MLA_V, seq), lambda b, g, i: (g, b)),
        ],
        out_specs=pl.BlockSpec((tq, hps * MLA_V), lambda b, g, i: (b * nq + i, g)),
        out_shape=jax.ShapeDtypeStruct((tokens, HEADS * MLA_V), BF16),
        scratch_shapes=[pltpu.VMEM((hps, 1, tq), F32),
                        pltpu.VMEM((hps, MLA_V + ONES_ROWS, tq), F32),
                        pltpu.VMEM((hps, 2, tq, tq), F32),
                        pltpu.VMEM((hps, 2, 1, tq), F32)],
        compiler_params=_params("arbitrary", "arbitrary", "arbitrary",
                                vmem_limit_bytes=ATTN_VMEM_LIMIT_BYTES),
        name="mla_attn",
    )(q, k, vt)


def _merge_kernel(h_ref, ada_ref, yh_ref, om_ref, sga_ref, sgb_ref, wbh_ref, wbm_ref, wo_ref,
                  lng_ref, lnb_ref, o_ref):
    gate = ada_ref[0, 5:6, :]
    parts = _row_parts(h_ref.shape[0], FFN_ROW_GROUPS)
    y_hg = [_dot(yh_ref[r, :], wbh_ref[...]) for r in parts]
    y_mla = [_dot(om_ref[r, :], wbm_ref[...]) for r in parts]
    merged = [(sga_ref[r, :].astype(F32) * a + sgb_ref[r, :].astype(F32) * b).astype(BF16)
              for r, a, b in zip(parts, y_hg, y_mla)]
    y = [_dot(m, wo_ref[...]) for m in merged]
    for r, yy in zip(parts, y):
        res = DN_ALPHA * h_ref[r, :] + (1.0 + gate) * yy
        o_ref[r, :] = _layer_norm(res, lng_ref[...], lnb_ref[...])


def _merge_call(h, ada, yh, om, sga, sgb, wbh, wbm, wo, ln_g, ln_b, *, layer, seq, tm):
    tokens = h.shape[0]
    row = pl.BlockSpec((tm, D_MODEL), lambda i: (i, 0))
    return pl.pallas_call(
        _merge_kernel,
        grid=(tokens // tm,),
        in_specs=[row, _ada_spec(layer, seq // tm), row, row, row, row,
                  _stacked(wbh, layer), _stacked(wbm, layer), _stacked(wo, layer),
                  _stacked(ln_g, 3 * layer + 1), _stacked(ln_b, 3 * layer + 1)],
        out_specs=row,
        out_shape=jax.ShapeDtypeStruct((tokens, D_MODEL), F32),
        compiler_params=_params("arbitrary"),
        name="merge_out",
    )(h, ada, yh, om, sga, sgb, wbh, wbm, wo, ln_g, ln_b)


def _mixer_weights(w_in, w_uq, w_ukv, q_norm_g, kv_norm_g):
    n = HEADS * HG_DK
    c_cq = 4 * n
    c_ckv = c_cq + Q_LORA
    c_kr = c_ckv + KV_LORA
    c_ga = c_kr + MLA_ROPE
    half = MLA_ROPE // 2
    w_in = w_in.astype(BF16)
    kr = w_in[:, :, c_kr:c_ga]
    kr_swapped = jnp.concatenate([kr[:, :, half:], kr[:, :, :half]], axis=2)
    pad = jnp.zeros((DEPTH, D_MODEL, 128 - MLA_ROPE), BF16)
    w_lat = jnp.concatenate([w_in[:, :, c_cq:c_kr], kr, pad, kr_swapped, pad], axis=2)

    uq = w_uq.astype(BF16).reshape(DEPTH, Q_LORA, HEADS, MLA_QK)
    uq_nope = uq[..., :MLA_NOPE].reshape(DEPTH, Q_LORA, HEADS * MLA_NOPE)
    uq_rope = uq[..., MLA_NOPE:]
    uq_rope_sw = jnp.concatenate([uq_rope[..., half:], uq_rope[..., :half]], axis=3)
    w_uq_p = jnp.concatenate([uq_nope, uq_rope.reshape(DEPTH, Q_LORA, HEADS * MLA_ROPE),
                              uq_rope_sw.reshape(DEPTH, Q_LORA, HEADS * MLA_ROPE)], axis=2)

    ukv = w_ukv.astype(BF16).reshape(DEPTH, KV_LORA, HEADS, MLA_NOPE + MLA_V)
    w_uk = ukv[..., :MLA_NOPE].reshape(DEPTH, KV_LORA, HEADS * MLA_NOPE)
    w_vt = jnp.swapaxes(ukv[..., MLA_NOPE:].reshape(DEPTH, KV_LORA, HEADS * MLA_V), 1, 2)
    return {
        "w_hg": w_in[:, :, :c_cq],
        "w_lat": w_lat,
        "w_gate": w_in[:, :, c_ga:],
        "w_uq": w_uq_p,
        "w_uk": w_uk,
        "w_vt": w_vt,
        "q_norm_g": q_norm_g.reshape(DEPTH, 1, Q_LORA),
        "kv_norm_g": kv_norm_g.reshape(DEPTH, 1, KV_LORA),
    }


def _tile(seq, want):
    t = min(want, seq)
    assert seq % t == 0, (seq, t)
    return t


def kernel(x, c, positions, ada_w, ada_b, ln_g, ln_b, ffn1_gate, ffn1_up, ffn1_down, w_in,
           hg_lower_bound, hg_norm_g, mla_q_norm_g, mla_w_uq, mla_kv_norm_g, mla_w_ukv,
           w_branch_hg, w_branch_mla, w_out, ffn2_gate, ffn2_up, ffn2_down):
    batch, seq, _ = x.shape
    tokens = batch * seq
    tm_ffn = _tile(seq, 256 * FFN_ROW_GROUPS)
    tm_in = _tile(seq, 512)
    tm_merge = _tile(seq, 256 * FFN_ROW_GROUPS)
    tb_hgrn = _tile(seq, 512)
    tq = _tile(seq, 512)
    attn_heads_per_step = 4

    ada = _ada_call(c, ada_w, ada_b).reshape(DEPTH, batch, N_ADA, D_MODEL)
    cos_t, sin_t = _rope_call(positions)
    lbraw = hg_lower_bound.astype(F32)

    ln_g3 = ln_g.reshape(DEPTH * 3, 1, D_MODEL)
    ln_b3 = ln_b.reshape(DEPTH * 3, 1, D_MODEL)
    ffn1 = [w.astype(BF16) for w in (ffn1_gate, ffn1_up, ffn1_down)]
    ffn2 = [w.astype(BF16) for w in (ffn2_gate, ffn2_up, ffn2_down)]
    w_mix = _mixer_weights(w_in, mla_w_uq, mla_w_ukv, mla_q_norm_g, mla_kv_norm_g)
    w_tail = [w.astype(BF16) for w in (w_branch_hg, w_branch_mla, w_out)]
    gnorm = hg_norm_g.reshape(DEPTH, 1, HEADS * HG_DV)

    h = x.reshape(tokens, D_MODEL)
    for l in range(DEPTH):
        h = _ffn_call(h, ada, *ffn1, ln_g3, ln_b3, layer=l, sub=0, seq=seq, tm=tm_ffn)
        hq, lf, hk, hv, og, q, k, vt, sga, sgb = _inproj_call(
            h, ada, w_mix, lbraw, cos_t, sin_t, layer=l, seq=seq, tm=tm_in)
        yh = _hgrn_call(hq, lf, hk, hv, og, gnorm, layer=l, batch=batch, seq=seq, tb=tb_hgrn)
        om = _attn_call(q, k, vt, batch=batch, seq=seq, tq=tq, hps=attn_heads_per_step)
        h = _merge_call(h, ada, yh, om, sga, sgb, *w_tail, ln_g3, ln_b3,
                        layer=l, seq=seq, tm=tm_merge)
        h = _ffn_call(h, ada, *ffn2, ln_g3, ln_b3, layer=l, sub=2, seq=seq, tm=tm_ffn)
    return h.reshape(batch, seq, D_MODEL)
```

```python
import functools

import jax
import jax.numpy as jnp
import numpy as np
from jax import lax
from jax.experimental import pallas as pl
from jax.experimental.pallas import tpu as pltpu

F32 = jnp.float32
BF16 = jnp.bfloat16

D_MODEL = 1024
DEPTH = 4
HEADS = 8
HG_DK = 128
HG_DV = 128
HG_CHUNK = 64
LB_FLOOR = 1e-30
MLA_NOPE = 128
MLA_ROPE = 64
MLA_QK = MLA_NOPE + MLA_ROPE
MLA_V = 128
Q_LORA = 384
KV_LORA = 256
ROPE_THETA = 10000.0
MASK_VALUE = -1e30
D_FF = 2816
DN_ALPHA = (2.0 * DEPTH) ** 0.25
LN_EPS = 1e-5
RMS_EPS = 1e-6
N_ADA = 9
LOG2E = 1.4426950408889634
ATTN_SCALE_LOG2E = float(MLA_QK) ** -0.5 * LOG2E
SUBLANES = 8
ONES_ROWS = 16
FFN_ROW_GROUPS = 4
HG_CHUNKS_PER_ITER = 8

VMEM_LIMIT_BYTES = 56 * 1024 * 1024
ATTN_VMEM_LIMIT_BYTES = 63 * 1024 * 1024


def _params(*sem, vmem_limit_bytes=VMEM_LIMIT_BYTES):
    return pltpu.CompilerParams(dimension_semantics=sem, vmem_limit_bytes=vmem_limit_bytes)


def _resident(shape):
    zeros = (0,) * len(shape)
    return pl.BlockSpec(shape, lambda *_: zeros, pipeline_mode=pl.Buffered(1))


def _stacked(arr, index):
    rest = (0,) * (arr.ndim - 1)
    return pl.BlockSpec((None,) + arr.shape[1:], lambda *_: (index,) + rest,
                        pipeline_mode=pl.Buffered(1))


def _silu(x):
    return x * jax.nn.sigmoid(x)


def _dot(a, b):
    return jnp.dot(a, b, preferred_element_type=F32)


def _dot_nt(a, b):
    return lax.dot_general(a, b, (((1,), (1,)), ((), ())), preferred_element_type=F32)


def _dot_tn(a, b):
    return lax.dot_general(a, b, (((0,), (0,)), ((), ())), preferred_element_type=F32)


def _row_parts(rows, n):
    assert rows % n == 0, (rows, n)
    return [slice(i * rows // n, (i + 1) * rows // n) for i in range(n)]


def _layer_norm(r, g, b):
    mu = jnp.mean(r, axis=-1, keepdims=True)
    d = r - mu
    var = jnp.mean(d * d, axis=-1, keepdims=True)
    return d * lax.rsqrt(var + LN_EPS) * g + b


def _rms_norm(x, g):
    return x * lax.rsqrt(jnp.mean(x * x, axis=-1, keepdims=True) + RMS_EPS) * g


def _ada_kernel(c_ref, w_ref, b_ref, o_ref):
    cond = _silu(c_ref[...])
    o_ref[0] = jnp.dot(cond, w_ref[0], preferred_element_type=F32,
                       precision=lax.Precision.HIGHEST) + b_ref[0]


def _ada_call(c, ada_w, ada_b):
    batch = c.shape[0]
    n_out = ada_w.shape[-1]
    tn = D_MODEL
    return pl.pallas_call(
        _ada_kernel,
        grid=(DEPTH, n_out // tn),
        in_specs=[
            pl.BlockSpec((batch, D_MODEL), lambda l, j: (0, 0)),
            pl.BlockSpec((1, D_MODEL, tn), lambda l, j: (l, 0, j)),
            pl.BlockSpec((1, 1, tn), lambda l, j: (l, 0, j)),
        ],
        out_specs=pl.BlockSpec((1, batch, tn), lambda l, j: (l, 0, j)),
        out_shape=jax.ShapeDtypeStruct((DEPTH, batch, n_out), F32),
        compiler_params=_params("arbitrary", "arbitrary"),
        name="ada",
    )(c, ada_w, ada_b.reshape(DEPTH, 1, n_out))


def _rope_kernel(pos_ref, inv_ref, sign_ref, cos_ref, sin_ref):
    ang = pos_ref[...].astype(F32) * inv_ref[...]
    cos_ref[...] = jnp.cos(ang)
    sin_ref[...] = jnp.sin(ang) * sign_ref[...]


def _rope_call(positions):
    tokens = positions.size
    tm = min(1024, tokens)
    half = MLA_ROPE // 2
    inv = 1.0 / (ROPE_THETA ** (jnp.arange(0, MLA_ROPE, 2, dtype=F32) / MLA_ROPE))
    inv_t = jnp.tile(inv, 4).reshape(1, 4 * half)
    sign = jnp.tile(jnp.concatenate([-jnp.ones((half,), F32), jnp.ones((half,), F32)]), 2)
    sign = sign.reshape(1, 4 * half)
    row = pl.BlockSpec((tm, 4 * half), lambda i: (i, 0))
    const = pl.BlockSpec((1, 4 * half), lambda i: (0, 0))
    return pl.pallas_call(
        _rope_kernel,
        grid=(tokens // tm,),
        in_specs=[pl.BlockSpec((tm, 1), lambda i: (i, 0)), const, const],
        out_specs=[row, row],
        out_shape=[jax.ShapeDtypeStruct((tokens, 4 * half), F32)] * 2,
        compiler_params=_params("arbitrary"),
        name="rope_tables",
    )(positions.reshape(tokens, 1), inv_t, sign)


def _ffn_kernel(h_ref, ada_ref, wg_ref, wu_ref, wd_ref, lng_ref, lnb_ref, o_ref, *, k0):
    shift = ada_ref[0, k0:k0 + 1, :]
    scale = ada_ref[0, k0 + 1:k0 + 2, :]
    gate = ada_ref[0, k0 + 2:k0 + 3, :]
    parts = _row_parts(h_ref.shape[0], FFN_ROW_GROUPS)
    u = [(h_ref[r, :] * (1.0 + scale) + shift).astype(BF16) for r in parts]
    act = []
    for uu in u:
        g = _dot(uu, wg_ref[...])
        up = _dot(uu, wu_ref[...])
        act.append((_silu(g) * up).astype(BF16))
    y = [_dot(a, wd_ref[...]) for a in act]
    for r, yy in zip(parts, y):
        res = DN_ALPHA * h_ref[r, :] + (0.5 * (1.0 + gate)) * yy
        o_ref[r, :] = _layer_norm(res, lng_ref[...], lnb_ref[...])


def _ada_spec(layer, per_b):
    return pl.BlockSpec((None, 1, N_ADA, D_MODEL), lambda i: (layer, i // per_b, 0, 0))


def _ffn_call(h, ada, wg, wu, wd, ln_g, ln_b, *, layer, sub, seq, tm):
    tokens = h.shape[0]
    row = pl.BlockSpec((tm, D_MODEL), lambda i: (i, 0))
    return pl.pallas_call(
        functools.partial(_ffn_kernel, k0=3 * sub),
        grid=(tokens // tm,),
        in_specs=[
            row,
            _ada_spec(layer, seq // tm),
            _stacked(wg, layer),
            _stacked(wu, layer),
            _stacked(wd, layer),
            _stacked(ln_g, 3 * layer + sub),
            _stacked(ln_b, 3 * layer + sub),
        ],
        out_specs=row,
        out_shape=jax.ShapeDtypeStruct((tokens, D_MODEL), F32),
        compiler_params=_params("arbitrary"),
        name="ffn",
    )(h, ada, wg, wu, wd, ln_g, ln_b)


def _inproj_kernel(h_ref, ada_ref, whg_ref, wlat_ref, wgate_ref, lbraw_ref, qng_ref, wuq_ref,
                   kvng_ref, wuk_ref, wvt_ref, cos_ref, sin_ref,
                   hq_ref, lf_ref, hk_ref, hv_ref, og_ref, q_ref, k_ref, vt_ref, sga_ref, sgb_ref,
                   *, layer):
    shift = ada_ref[0, 3:4, :]
    scale = ada_ref[0, 4:5, :]

    raw = lbraw_ref[...]
    e = jnp.exp(raw - jnp.max(raw, axis=0, keepdims=True))
    lb = jnp.zeros((1, HEADS * HG_DK), F32)
    for i in range(1, layer + 1):
        lb = lb + e[i:i + 1, :]
    lb = lb / jnp.sum(e, axis=0, keepdims=True)
    one_minus_lb = 1.0 - lb

    def project(rows):
        u = (h_ref[rows, :] * (1.0 + scale) + shift).astype(BF16)
        n = HEADS * HG_DK
        zf = _dot(u, whg_ref[:, n:2 * n])
        t = jnp.exp(-jnp.abs(zf))
        r = 1.0 / (1.0 + t)
        tr = t * r
        pos = zf >= 0.0
        sig_pos = jnp.where(pos, r, tr)
        sig_neg = jnp.where(pos, tr, r)
        lf_ref[rows, :] = jnp.log2(jnp.maximum(lb, LB_FLOOR) + one_minus_lb * sig_pos)
        hk_ref[rows, :] = (one_minus_lb * sig_neg).astype(BF16)

        zgate = _dot(u, wgate_ref[...])
        sga_ref[rows, :] = jax.nn.sigmoid(zgate[:, 0:D_MODEL]).astype(sga_ref.dtype)
        sgb_ref[rows, :] = jax.nn.sigmoid(zgate[:, D_MODEL:2 * D_MODEL]).astype(sgb_ref.dtype)

        zl = _dot(u, wlat_ref[...])
        cos_t = cos_ref[rows, :]
        sin_t = sin_ref[rows, :]
        cqn = _rms_norm(zl[:, 0:Q_LORA], qng_ref[...]).astype(BF16)
        qf = _dot(cqn, wuq_ref[...]) * ATTN_SCALE_LOG2E
        nq = HEADS * MLA_NOPE
        nr = HEADS * MLA_ROPE
        cos_q = jnp.concatenate([cos_t] * (nr // 128), axis=1)
        sin_q = jnp.concatenate([sin_t] * (nr // 128), axis=1)
        q_rope = qf[:, nq:nq + nr] * cos_q + qf[:, nq + nr:nq + 2 * nr] * sin_q
        ckvn = _rms_norm(zl[:, Q_LORA:Q_LORA + KV_LORA], kvng_ref[...]).astype(BF16)
        k_nope = _dot(ckvn, wuk_ref[...])
        vt_ref[:, rows] = _dot_nt(wvt_ref[...], ckvn).astype(BF16)
        c0 = Q_LORA + KV_LORA
        k_rope = (zl[:, c0:c0 + 128] * cos_t + zl[:, c0 + 128:c0 + 256] * sin_t)[:, 0:MLA_ROPE]
        k_rope = k_rope.astype(BF16)
        for hh in range(HEADS):
            q_ref[hh, rows, 0:MLA_NOPE] = qf[:, hh * MLA_NOPE:(hh + 1) * MLA_NOPE].astype(BF16)
            q_ref[hh, rows, MLA_NOPE:MLA_QK] = (
                q_rope[:, hh * MLA_ROPE:(hh + 1) * MLA_ROPE].astype(BF16))
            k_ref[hh, rows, 0:MLA_NOPE] = k_nope[:, hh * MLA_NOPE:(hh + 1) * MLA_NOPE].astype(BF16)
            k_ref[hh, rows, MLA_NOPE:MLA_QK] = k_rope

        hq_ref[rows, :] = _silu(_dot(u, whg_ref[:, 0:n])).astype(BF16)
        og_ref[rows, :] = _silu(_dot(u, whg_ref[:, 3 * n:4 * n])).astype(og_ref.dtype)
        hv_ref[rows, :] = _dot(u, whg_ref[:, 2 * n:3 * n]).astype(BF16)

    for rows in _row_parts(h_ref.shape[0], 2):
        project(rows)


def _inproj_call(h, ada, w, lbraw, cos_t, sin_t, *, layer, seq, tm):
    tokens = h.shape[0]
    row = pl.BlockSpec((tm, D_MODEL), lambda i: (i, 0))
    tab = pl.BlockSpec((tm, 128), lambda i: (i, 0))
    headed = pl.BlockSpec((HEADS, tm, MLA_QK), lambda i: (0, i, 0))
    wide = jax.ShapeDtypeStruct((tokens, D_MODEL), BF16)
    qk = jax.ShapeDtypeStruct((HEADS, tokens, MLA_QK), BF16)
    return pl.pallas_call(
        functools.partial(_inproj_kernel, layer=layer),
        grid=(tokens // tm,),
        in_specs=[
            row,
            _ada_spec(layer, seq // tm),
            _stacked(w["w_hg"], layer),
            _stacked(w["w_lat"], layer),
            _stacked(w["w_gate"], layer),
            _resident(lbraw.shape),
            _stacked(w["q_norm_g"], layer),
            _stacked(w["w_uq"], layer),
            _stacked(w["kv_norm_g"], layer),
            _stacked(w["w_uk"], layer),
            _stacked(w["w_vt"], layer),
            tab, tab,
        ],
        out_specs=[row, row, row, row, row, headed, headed,
                   pl.BlockSpec((HEADS * MLA_V, tm), lambda i: (0, i)), row, row],
        out_shape=[wide, jax.ShapeDtypeStruct((tokens, D_MODEL), F32), wide, wide, wide,
                   qk, qk, jax.ShapeDtypeStruct((HEADS * MLA_V, tokens), BF16), wide, wide],
        compiler_params=_params("arbitrary"),
        name="inproj",
    )(h, ada, w["w_hg"], w["w_lat"], w["w_gate"], lbraw, w["q_norm_g"], w["w_uq"],
      w["kv_norm_g"], w["w_uk"], w["w_vt"], cos_t, sin_t)


def _hgrn_masks():
    t = np.arange(HG_CHUNK)[:, None]
    s = np.arange(HG_CHUNK)[None, :]
    masks = [(t == s)]
    m = 1
    while m < HG_CHUNK:
        masks.append((t // (2 * m) == s // (2 * m)) & ((t & m) != 0) & ((s & m) == 0))
        m *= 2
    return np.stack(masks).astype(np.float32)


def _hgrn_kernel(q_ref, lf_ref, k_ref, v_ref, og_ref, gn_ref, mask_ref, o_ref, state_ref, *, n_chunks):
    @pl.when(pl.program_id(1) == 0)
    def _():
        state_ref[...] = jnp.zeros_like(state_ref)

    n_grp = HG_CHUNK // SUBLANES
    sub = lax.broadcasted_iota(jnp.int32, (SUBLANES, HG_DK), 0)
    sign_small = {m: jnp.where((sub & m) != 0, 1.0, -1.0) for m in (1, 2, 4)}

    def bcast_row(x, i):
        return jnp.broadcast_to(x[i:i + 1, :], (SUBLANES, HG_DK))

    def prepare(rows, hh):
        cols = slice(hh * HG_DK, (hh + 1) * HG_DK)
        q_bf = q_ref[rows, cols]
        k_bf = k_ref[rows, cols]
        q = q_bf.astype(F32)
        k = k_bf.astype(F32)

        lf = lf_ref[rows, cols]
        grp = [lf[SUBLANES * r:SUBLANES * (r + 1)] for r in range(n_grp)]
        for sh in (1, 2, 4):
            grp = [p + jnp.where(sub >= sh, pltpu.roll(p, sh, axis=0), 0.0) for p in grp]
        g_grp, ends = [], []
        for r in range(n_grp):
            gp = grp[r] if r == 0 else grp[r] + ends[r - 1]
            g_grp.append(gp)
            ends.append(bcast_row(gp, SUBLANES - 1))
        q_dec = (q * jnp.exp2(jnp.concatenate(g_grp, axis=0))).astype(BF16)
        diag = _dot_nt(q_bf, k_bf) * mask_ref[0]
        return dict(q=q, k=k, g_grp=g_grp, ends=ends, q_dec=q_dec, scores=diag)

    def level(hd, m, lvl):
        g_grp, ends = hd["g_grp"], hd["ends"]
        if m >= SUBLANES:
            gm = m // SUBLANES
            refs = [ends[(r // (2 * gm)) * 2 * gm + gm - 1] for r in range(n_grp)]
            args = [gp - rf if (SUBLANES * r) & m else rf - gp
                    for r, (gp, rf) in enumerate(zip(g_grp, refs))]
        else:
            if m == 4:
                refs = [bcast_row(gp, 3) for gp in g_grp]
            elif m == 2:
                refs = [jnp.where(sub < 4, bcast_row(gp, 1), bcast_row(gp, 5)) for gp in g_grp]
            else:
                refs = [jnp.where((sub & 1) != 0, pltpu.roll(gp, 1, axis=0), gp) for gp in g_grp]
            args = [(gp - rf) * sign_small[m] for gp, rf in zip(g_grp, refs)]
        if m >= SUBLANES:
            zero = jnp.zeros((SUBLANES, HG_DK), F32)
            q_rows, k_rows = [], []
            for r, a in enumerate(args):
                rows_r = slice(SUBLANES * r, SUBLANES * (r + 1))
                e_r = jnp.exp2(a)
                is_query = bool((SUBLANES * r) & m)
                q_rows.append(hd["q"][rows_r] * e_r if is_query else zero)
                k_rows.append(zero if is_query else hd["k"][rows_r] * e_r)
            q_side = jnp.concatenate(q_rows, axis=0)
            k_side = jnp.concatenate(k_rows, axis=0)
        else:
            e = jnp.exp2(jnp.concatenate(args, axis=0))
            q_side = hd["q"] * e
            k_side = hd["k"] * e
        part = _dot_nt(q_side.astype(BF16), k_side.astype(BF16))
        if 2 * m == HG_CHUNK:
            hd["scores"] = hd["scores"] + part
        else:
            hd["scores"] = hd["scores"] + part * mask_ref[lvl]

    def finish(rows, hh, hd):
        cols = slice(hh * HG_DK, (hh + 1) * HG_DK)
        v = v_ref[rows, cols]
        state = state_ref[hh]
        o = _dot_nt(hd["q_dec"], state.astype(BF16)) + _dot(hd["scores"].astype(BF16), v)
        g_end = hd["ends"][n_grp - 1]
        to_end = jnp.exp2(jnp.concatenate([g_end - gp for gp in hd["g_grp"]], axis=0))
        k_dec = (hd["k"] * to_end).astype(BF16)
        state_ref[hh] = jnp.exp2(g_end[0:1, :]) * state + _dot_tn(v, k_dec)
        y = _rms_norm(o, gn_ref[:, cols]) * og_ref[rows, cols].astype(F32)
        o_ref[rows, cols] = y.astype(o_ref.dtype)

    def chunk_group(c, carry):
        rows = [pl.ds(pl.multiple_of((HG_CHUNKS_PER_ITER * c + i) * HG_CHUNK, HG_CHUNK), HG_CHUNK)
                for i in range(HG_CHUNKS_PER_ITER)]
        work = [(r, hh, prepare(r, hh)) for r in rows for hh in range(HEADS)]
        m, lvl = 1, 1
        while m < HG_CHUNK:
            for _, _, hd in work:
                level(hd, m, lvl)
            m *= 2
            lvl += 1
        for r, hh, hd in work:
            finish(r, hh, hd)
        return carry

    assert n_chunks % HG_CHUNKS_PER_ITER == 0, n_chunks
    lax.fori_loop(0, n_chunks // HG_CHUNKS_PER_ITER, chunk_group, 0)


def _hgrn_call(hq, lf, hk, hv, og, gnorm, *, layer, batch, seq, tb):
    tokens = hq.shape[0]
    per_b = seq // tb
    width = HEADS * HG_DK
    blk = pl.BlockSpec((tb, width), lambda b, i: (b * per_b + i, 0))
    masks = jnp.asarray(_hgrn_masks())
    return pl.pallas_call(
        functools.partial(_hgrn_kernel, n_chunks=tb // HG_CHUNK),
        grid=(batch, per_b),
        in_specs=[blk, blk, blk, blk, blk, _stacked(gnorm, layer), _resident(masks.shape)],
        out_specs=blk,
        out_shape=jax.ShapeDtypeStruct((tokens, width), BF16),
        scratch_shapes=[pltpu.VMEM((HEADS, HG_DV, HG_DK), F32)],
        compiler_params=_params("arbitrary", "arbitrary"),
        name="hgrn",
    )(hq, lf, hk, hv, og, gnorm, masks)


def _attn_kernel(q_ref, k_ref, vt_ref, o_ref, m_ref, acc_ref, s_ref, bmax_ref, *, tq, hps):
    qi = pl.program_id(2)
    m_ref[...] = jnp.full_like(m_ref, -jnp.inf)
    acc_ref[...] = jnp.zeros_like(acc_ref)
    ones = jnp.ones((ONES_ROWS, tq), BF16)

    def scores(a, j, slot):
        start = pl.multiple_of(j * tq, tq)
        s_t = _dot_nt(k_ref[a, pl.ds(start, tq), :], q_ref[a])
        s_ref[a, slot] = s_t
        bmax_ref[a, slot] = jnp.max(s_t, axis=0, keepdims=True)

    def fold(a, s_t, block_max, v_ext, qcols):
        m_old = m_ref[a, :, qcols]
        m_new = jnp.maximum(m_old, block_max)
        p = jnp.exp2((s_t - m_new).astype(BF16))
        alpha = jnp.exp2(m_old - m_new)
        acc_ref[a, :, qcols] = alpha * acc_ref[a, :, qcols] + _dot(v_ext, p)
        m_ref[a, :, qcols] = m_new

    def v_rows(a, start, n_keys):
        v_t = vt_ref[a * MLA_V:(a + 1) * MLA_V, pl.ds(start, n_keys)]
        return jnp.concatenate([v_t, ones[:, 0:n_keys]], axis=0)

    def accumulate(a, j, slot):
        start = pl.multiple_of(j * tq, tq)
        fold(a, s_ref[a, slot], bmax_ref[a, slot], v_rows(a, start, tq), slice(None))

    def accumulate_diagonal(a, slot):
        start = pl.multiple_of(qi * tq, tq)
        half = tq // 2
        for qcols, n_keys in ((slice(0, half), half), (slice(half, tq), tq)):
            kpos = lax.broadcasted_iota(jnp.int32, (n_keys, half), 0)
            qpos = lax.broadcasted_iota(jnp.int32, (n_keys, half), 1) + qcols.start
            s_t = jnp.where(kpos <= qpos, s_ref[a, slot, 0:n_keys, qcols], MASK_VALUE)
            fold(a, s_t, jnp.max(s_t, axis=0, keepdims=True), v_rows(a, start, n_keys), qcols)

    for a in range(hps):
        scores(a, 0, 0)

    def step(j, slot):
        for a in range(hps):
            scores(a, j + 1, 1 - slot)
            accumulate(a, j, slot)

    def body(jj, carry):
        step(2 * jj, 0)
        step(2 * jj + 1, 1)
        return carry

    lax.fori_loop(0, qi // 2, body, 0)

    @pl.when(qi % 2 == 1)
    def _():
        step(qi - 1, 0)
        for a in range(hps):
            accumulate_diagonal(a, 1)

    @pl.when(qi % 2 == 0)
    def _():
        for a in range(hps):
            accumulate_diagonal(a, 0)

    for a in range(hps):
        acc = acc_ref[a]
        o_t = acc[0:MLA_V] * (1.0 / acc[MLA_V:MLA_V + 1])
        o_ref[:, a * MLA_V:(a + 1) * MLA_V] = o_t.T.astype(o_ref.dtype)


def _attn_call(q, k, vt, *, batch, seq, tq, hps):
    tokens = vt.shape[1]
    nq = seq // tq
    return pl.pallas_call(
        functools.partial(_attn_kernel, tq=tq, hps=hps),
        grid=(batch, HEADS // hps, nq),
        in_specs=[
            pl.BlockSpec((hps, tq, MLA_QK), lambda b, g, i: (g, b * nq + i, 0)),
            pl.BlockSpec((hps, seq, MLA_QK), lambda b, g, i: (g, b, 0)),
            pl.BlockSpec((hps * MLA_V, seq), lambda b, g, i: (g, b)),
        ],
        out_specs=pl.BlockSpec((tq, hps * MLA_V), lambda b, g, i: (b * nq + i, g)),
        out_shape=jax.ShapeDtypeStruct((tokens, HEADS * MLA_V), BF16),
        scratch_shapes=[pltpu.VMEM((hps, 1, tq), F32),
                        pltpu.VMEM((hps, MLA_V + ONES_ROWS, tq), F32),
                        pltpu.VMEM((hps, 2, tq, tq), F32),
                        pltpu.VMEM((hps, 2, 1, tq), F32)],
        compiler_params=_params("arbitrary", "arbitrary", "arbitrary",
                                vmem_limit_bytes=ATTN_VMEM_LIMIT_BYTES),
        name="mla_attn",
    )(q, k, vt)


def _merge_kernel(h_ref, ada_ref, yh_ref, om_ref, sga_ref, sgb_ref, wbh_ref, wbm_ref, wo_ref,
                  lng_ref, lnb_ref, o_ref):
    gate = ada_ref[0, 5:6, :]
    parts = _row_parts(h_ref.shape[0], FFN_ROW_GROUPS)
    y_hg = [_dot(yh_ref[r, :], wbh_ref[...]) for r in parts]
    y_mla = [_dot(om_ref[r, :], wbm_ref[...]) for r in parts]
    merged = [(sga_ref[r, :].astype(F32) * a + sgb_ref[r, :].astype(F32) * b).astype(BF16)
              for r, a, b in zip(parts, y_hg, y_mla)]
    y = [_dot(m, wo_ref[...]) for m in merged]
    for r, yy in zip(parts, y):
        res = DN_ALPHA * h_ref[r, :] + (1.0 + gate) * yy
        o_ref[r, :] = _layer_norm(res, lng_ref[...], lnb_ref[...])


def _merge_call(h, ada, yh, om, sga, sgb, wbh, wbm, wo, ln_g, ln_b, *, layer, seq, tm):
    tokens = h.shape[0]
    row = pl.BlockSpec((tm, D_MODEL), lambda i: (i, 0))
    return pl.pallas_call(
        _merge_kernel,
        grid=(tokens // tm,),
        in_specs=[row, _ada_spec(layer, seq // tm), row, row, row, row,
                  _stacked(wbh, layer), _stacked(wbm, layer), _stacked(wo, layer),
                  _stacked(ln_g, 3 * layer + 1), _stacked(ln_b, 3 * layer + 1)],
        out_specs=row,
        out_shape=jax.ShapeDtypeStruct((tokens, D_MODEL), F32),
        compiler_params=_params("arbitrary"),
        name="merge_out",
    )(h, ada, yh, om, sga, sgb, wbh, wbm, wo, ln_g, ln_b)


def _mixer_weights(w_in, w_uq, w_ukv, q_norm_g, kv_norm_g):
    n = HEADS * HG_DK
    c_cq = 4 * n
    c_ckv = c_cq + Q_LORA
    c_kr = c_ckv + KV_LORA
    c_ga = c_kr + MLA_ROPE
    half = MLA_ROPE // 2
    w_in = w_in.astype(BF16)
    kr = w_in[:, :, c_kr:c_ga]
    kr_swapped = jnp.concatenate([kr[:, :, half:], kr[:, :, :half]], axis=2)
    pad = jnp.zeros((DEPTH, D_MODEL, 128 - MLA_ROPE), BF16)
    w_lat = jnp.concatenate([w_in[:, :, c_cq:c_kr], kr, pad, kr_swapped, pad], axis=2)

    uq = w_uq.astype(BF16).reshape(DEPTH, Q_LORA, HEADS, MLA_QK)
    uq_nope = uq[..., :MLA_NOPE].reshape(DEPTH, Q_LORA, HEADS * MLA_NOPE)
    uq_rope = uq[..., MLA_NOPE:]
    uq_rope_sw = jnp.concatenate([uq_rope[..., half:], uq_rope[..., :half]], axis=3)
    w_uq_p = jnp.concatenate([uq_nope, uq_rope.reshape(DEPTH, Q_LORA, HEADS * MLA_ROPE),
                              uq_rope_sw.reshape(DEPTH, Q_LORA, HEADS * MLA_ROPE)], axis=2)

    ukv = w_ukv.astype(BF16).reshape(DEPTH, KV_LORA, HEADS, MLA_NOPE + MLA_V)
    w_uk = ukv[..., :MLA_NOPE].reshape(DEPTH, KV_LORA, HEADS * MLA_NOPE)
    w_vt = jnp.swapaxes(ukv[..., MLA_NOPE:].reshape(DEPTH, KV_LORA, HEADS * MLA_V), 1, 2)
    return {
        "w_hg": w_in[:, :, :c_cq],
        "w_lat": w_lat,
        "w_gate": w_in[:, :, c_ga:],
        "w_uq": w_uq_p,
        "w_uk": w_uk,
        "w_vt": w_vt,
        "q_norm_g": q_norm_g.reshape(DEPTH, 1, Q_LORA),
        "kv_norm_g": kv_norm_g.reshape(DEPTH, 1, KV_LORA),
    }


def _tile(seq, want):
    t = min(want, seq)
    assert seq % t == 0, (seq, t)
    return t


def kernel(x, c, positions, ada_w, ada_b, ln_g, ln_b, ffn1_gate, ffn1_up, ffn1_down, w_in,
           hg_lower_bound, hg_norm_g, mla_q_norm_g, mla_w_uq, mla_kv_norm_g, mla_w_ukv,
           w_branch_hg, w_branch_mla, w_out, ffn2_gate, ffn2_up, ffn2_down):
    batch, seq, _ = x.shape
    tokens = batch * seq
    tm_ffn = _tile(seq, 256 * FFN_ROW_GROUPS)
    tm_in = _tile(seq, 512)
    tm_merge = _tile(seq, 256 * FFN_ROW_GROUPS)
    tb_hgrn = _tile(seq, 512)
    tq = _tile(seq, 512)
    attn_heads_per_step = 4

    ada = _ada_call(c, ada_w, ada_b).reshape(DEPTH, batch, N_ADA, D_MODEL)
    cos_t, sin_t = _rope_call(positions)
    lbraw = hg_lower_bound.astype(F32)

    ln_g3 = ln_g.reshape(DEPTH * 3, 1, D_MODEL)
    ln_b3 = ln_b.reshape(DEPTH * 3, 1, D_MODEL)
    ffn1 = [w.astype(BF16) for w in (ffn1_gate, ffn1_up, ffn1_down)]
    ffn2 = [w.astype(BF16) for w in (ffn2_gate, ffn2_up, ffn2_down)]
    w_mix = _mixer_weights(w_in, mla_w_uq, mla_w_ukv, mla_q_norm_g, mla_kv_norm_g)
    w_tail = [w.astype(BF16) for w in (w_branch_hg, w_branch_mla, w_out)]
    gnorm = hg_norm_g.reshape(DEPTH, 1, HEADS * HG_DV)

    h = x.reshape(tokens, D_MODEL)
    for l in range(DEPTH):
        h = _ffn_call(h, ada, *ffn1, ln_g3, ln_b3, layer=l, sub=0, seq=seq, tm=tm_ffn)
        hq, lf, hk, hv, og, q, k, vt, sga, sgb = _inproj_call(
            h, ada, w_mix, lbraw, cos_t, sin_t, layer=l, seq=seq, tm=tm_in)
        yh = _hgrn_call(hq, lf, hk, hv, og, gnorm, layer=l, batch=batch, seq=seq, tb=tb_hgrn)
        om = _attn_call(q, k, vt, batch=batch, seq=seq, tq=tq, hps=attn_heads_per_step)
        h = _merge_call(h, ada, yh, om, sga, sgb, *w_tail, ln_g3, ln_b3,
                        layer=l, seq=seq, tm=tm_merge)
        h = _ffn_call(h, ada, *ffn2, ln_g3, ln_b3, layer=l, sub=2, seq=seq, tm=tm_ffn)
    return h.reshape(batch, seq, D_MODEL)
```

```python
import functools

import jax
import jax.numpy as jnp
import numpy as np
from jax import lax
from jax.experimental import pallas as pl
from jax.experimental.pallas import tpu as pltpu

F32 = jnp.float32
BF16 = jnp.bfloat16

D_MODEL = 1024
DEPTH = 4
HEADS = 8
HG_DK = 128
HG_DV = 128
HG_CHUNK = 64
LB_FLOOR = 1e-30
MLA_NOPE = 128
MLA_ROPE = 64
MLA_QK = MLA_NOPE + MLA_ROPE
MLA_V = 128
Q_LORA = 384
KV_LORA = 256
ROPE_THETA = 10000.0
MASK_VALUE = -1e30
D_FF = 2816
DN_ALPHA = (2.0 * DEPTH) ** 0.25
LN_EPS = 1e-5
RMS_EPS = 1e-6
N_ADA = 9
LOG2E = 1.4426950408889634
ATTN_SCALE_LOG2E = float(MLA_QK) ** -0.5 * LOG2E
SUBLANES = 8
ONES_ROWS = 16
FFN_ROW_GROUPS = 4
HG_CHUNKS_PER_ITER = 8

VMEM_LIMIT_BYTES = 56 * 1024 * 1024
ATTN_VMEM_LIMIT_BYTES = 63 * 1024 * 1024


def _params(*sem, vmem_limit_bytes=VMEM_LIMIT_BYTES):
    return pltpu.CompilerParams(dimension_semantics=sem, vmem_limit_bytes=vmem_limit_bytes)


def _resident(shape):
    zeros = (0,) * len(shape)
    return pl.BlockSpec(shape, lambda *_: zeros, pipeline_mode=pl.Buffered(1))


def _stacked(arr, index):
    rest = (0,) * (arr.ndim - 1)
    return pl.BlockSpec((None,) + arr.shape[1:], lambda *_: (index,) + rest,
                        pipeline_mode=pl.Buffered(1))


def _silu(x):
    return x * jax.nn.sigmoid(x)


def _dot(a, b):
    return jnp.dot(a, b, preferred_element_type=F32)


def _dot_nt(a, b):
    return lax.dot_general(a, b, (((1,), (1,)), ((), ())), preferred_element_type=F32)


def _dot_tn(a, b):
    return lax.dot_general(a, b, (((0,), (0,)), ((), ())), preferred_element_type=F32)


def _row_parts(rows, n):
    assert rows % n == 0, (rows, n)
    return [slice(i * rows // n, (i + 1) * rows // n) for i in range(n)]


def _layer_norm(r, g, b):
    mu = jnp.mean(r, axis=-1, keepdims=True)
    d = r - mu
    var = jnp.mean(d * d, axis=-1, keepdims=True)
    return d * lax.rsqrt(var + LN_EPS) * g + b


def _rms_norm(x, g):
    return x * lax.rsqrt(jnp.mean(x * x, axis=-1, keepdims=True) + RMS_EPS) * g


def _ada_kernel(c_ref, w_ref, b_ref, o_ref):
    cond = _silu(c_ref[...])
    o_ref[0] = jnp.dot(cond, w_ref[0], preferred_element_type=F32,
                       precision=lax.Precision.HIGHEST) + b_ref[0]


def _ada_call(c, ada_w, ada_b):
    batch = c.shape[0]
    n_out = ada_w.shape[-1]
    tn = D_MODEL
    return pl.pallas_call(
        _ada_kernel,
        grid=(DEPTH, n_out // tn),
        in_specs=[
            pl.BlockSpec((batch, D_MODEL), lambda l, j: (0, 0)),
            pl.BlockSpec((1, D_MODEL, tn), lambda l, j: (l, 0, j)),
            pl.BlockSpec((1, 1, tn), lambda l, j: (l, 0, j)),
        ],
        out_specs=pl.BlockSpec((1, batch, tn), lambda l, j: (l, 0, j)),
        out_shape=jax.ShapeDtypeStruct((DEPTH, batch, n_out), F32),
        compiler_params=_params("arbitrary", "arbitrary"),
        name="ada",
    )(c, ada_w, ada_b.reshape(DEPTH, 1, n_out))


def _rope_kernel(pos_ref, inv_ref, sign_ref, cos_ref, sin_ref):
    ang = pos_ref[...].astype(F32) * inv_ref[...]
    cos_ref[...] = jnp.cos(ang)
    sin_ref[...] = jnp.sin(ang) * sign_ref[...]


def _rope_call(positions):
    tokens = positions.size
    tm = min(1024, tokens)
    half = MLA_ROPE // 2
    inv = 1.0 / (ROPE_THETA ** (jnp.arange(0, MLA_ROPE, 2, dtype=F32) / MLA_ROPE))
    inv_t = jnp.tile(inv, 4).reshape(1, 4 * half)
    sign = jnp.tile(jnp.concatenate([-jnp.ones((half,), F32), jnp.ones((half,), F32)]), 2)
    sign = sign.reshape(1, 4 * half)
    row = pl.BlockSpec((tm, 4 * half), lambda i: (i, 0))
    const = pl.BlockSpec((1, 4 * half), lambda i: (0, 0))
    return pl.pallas_call(
        _rope_kernel,
        grid=(tokens // tm,),
        in_specs=[pl.BlockSpec((tm, 1), lambda i: (i, 0)), const, const],
        out_specs=[row, row],
        out_shape=[jax.ShapeDtypeStruct((tokens, 4 * half), F32)] * 2,
        compiler_params=_params("arbitrary"),
        name="rope_tables",
    )(positions.reshape(tokens, 1), inv_t, sign)


def _ffn_kernel(h_ref, ada_ref, wg_ref, wu_ref, wd_ref, lng_ref, lnb_ref, o_ref, *, k0):
    shift = ada_ref[0, k0:k0 + 1, :]
    scale = ada_ref[0, k0 + 1:k0 + 2, :]
    gate = ada_ref[0, k0 + 2:k0 + 3, :]
    parts = _row_parts(h_ref.shape[0], FFN_ROW_GROUPS)
    u = [(h_ref[r, :] * (1.0 + scale) + shift).astype(BF16) for r in parts]
    act = []
    for uu in u:
        g = _dot(uu, wg_ref[...])
        up = _dot(uu, wu_ref[...])
        act.append((_silu(g) * up).astype(BF16))
    y = [_dot(a, wd_ref[...]) for a in act]
    for r, yy in zip(parts, y):
        res = DN_ALPHA * h_ref[r, :] + (0.5 * (1.0 + gate)) * yy
        o_ref[r, :] = _layer_norm(res, lng_ref[...], lnb_ref[...])


def _ada_spec(layer, per_b):
    return pl.BlockSpec((None, 1, N_ADA, D_MODEL), lambda i: (layer, i // per_b, 0, 0))


def _ffn_call(h, ada, wg, wu, wd, ln_g, ln_b, *, layer, sub, seq, tm):
    tokens = h.shape[0]
    row = pl.BlockSpec((tm, D_MODEL), lambda i: (i, 0))
    return pl.pallas_call(
        functools.partial(_ffn_kernel, k0=3 * sub),
        grid=(tokens // tm,),
        in_specs=[
            row,
            _ada_spec(layer, seq // tm),
            _stacked(wg, layer),
            _stacked(wu, layer),
            _stacked(wd, layer),
            _stacked(ln_g, 3 * layer + sub),
            _stacked(ln_b, 3 * layer + sub),
        ],
        out_specs=row,
        out_shape=jax.ShapeDtypeStruct((tokens, D_MODEL), F32),
        compiler_params=_params("arbitrary"),
        name="ffn",
    )(h, ada, wg, wu, wd, ln_g, ln_b)


def _inproj_kernel(h_ref, ada_ref, whg_ref, wlat_ref, wgate_ref, lbraw_ref, qng_ref, wuq_ref,
                   kvng_ref, wuk_ref, wvt_ref, cos_ref, sin_ref,
                   hq_ref, lf_ref, hk_ref, hv_ref, og_ref, q_ref, k_ref, vt_ref, sga_ref, sgb_ref,
                   *, layer):
    shift = ada_ref[0, 3:4, :]
    scale = ada_ref[0, 4:5, :]

    raw = lbraw_ref[...]
    e = jnp.exp(raw - jnp.max(raw, axis=0, keepdims=True))
    lb = jnp.zeros((1, HEADS * HG_DK), F32)
    for i in range(1, layer + 1):
        lb = lb + e[i:i + 1, :]
    lb = lb / jnp.sum(e, axis=0, keepdims=True)
    one_minus_lb = 1.0 - lb

    def project(rows):
        u = (h_ref[rows, :] * (1.0 + scale) + shift).astype(BF16)
        n = HEADS * HG_DK
        zf = _dot(u, whg_ref[:, n:2 * n])
        t = jnp.exp(-jnp.abs(zf))
        r = 1.0 / (1.0 + t)
        tr = t * r
        pos = zf >= 0.0
        sig_pos = jnp.where(pos, r, tr)
        sig_neg = jnp.where(pos, tr, r)
        lf_ref[rows, :] = jnp.log2(jnp.maximum(lb, LB_FLOOR) + one_minus_lb * sig_pos)
        hk_ref[rows, :] = (one_minus_lb * sig_neg).astype(BF16)

        zgate = _dot(u, wgate_ref[...])
        sga_ref[rows, :] = jax.nn.sigmoid(zgate[:, 0:D_MODEL]).astype(sga_ref.dtype)
        sgb_ref[rows, :] = jax.nn.sigmoid(zgate[:, D_MODEL:2 * D_MODEL]).astype(sgb_ref.dtype)

        zl = _dot(u, wlat_ref[...])
        cos_t = cos_ref[rows, :]
        sin_t = sin_ref[rows, :]
        cqn = _rms_norm(zl[:, 0:Q_LORA], qng_ref[...]).astype(BF16)
        qf = _dot(cqn, wuq_ref[...]) * ATTN_SCALE_LOG2E
        nq = HEADS * MLA_NOPE
        nr = HEADS * MLA_ROPE
        cos_q = jnp.concatenate([cos_t] * (nr // 128), axis=1)
        sin_q = jnp.concatenate([sin_t] * (nr // 128), axis=1)
        q_rope = qf[:, nq:nq + nr] * cos_q + qf[:, nq + nr:nq + 2 * nr] * sin_q
        ckvn = _rms_norm(zl[:, Q_LORA:Q_LORA + KV_LORA], kvng_ref[...]).astype(BF16)
        k_nope = _dot(ckvn, wuk_ref[...])
        vt_ref[:, rows] = _dot_nt(wvt_ref[...], ckvn).astype(BF16)
        c0 = Q_LORA + KV_LORA
        k_rope = (zl[:, c0:c0 + 128] * cos_t + zl[:, c0 + 128:c0 + 256] * sin_t)[:, 0:MLA_ROPE]
        k_rope = k_rope.astype(BF16)
        for hh in range(HEADS):
            q_ref[hh, rows, 0:MLA_NOPE] = qf[:, hh * MLA_NOPE:(hh + 1) * MLA_NOPE].astype(BF16)
            q_ref[hh, rows, MLA_NOPE:MLA_QK] = (
                q_rope[:, hh * MLA_ROPE:(hh + 1) * MLA_ROPE].astype(BF16))
            k_ref[hh, rows, 0:MLA_NOPE] = k_nope[:, hh * MLA_NOPE:(hh + 1) * MLA_NOPE].astype(BF16)
            k_ref[hh, rows, MLA_NOPE:MLA_QK] = k_rope

        hq_ref[rows, :] = _silu(_dot(u, whg_ref[:, 0:n])).astype(BF16)
        og_ref[rows, :] = _silu(_dot(u, whg_ref[:, 3 * n:4 * n])).astype(og_ref.dtype)
        hv_ref[rows, :] = _dot(u, whg_ref[:, 2 * n:3 * n]).astype(BF16)

    for rows in _row_parts(h_ref.shape[0], 2):
        project(rows)


def _inproj_call(h, ada, w, lbraw, cos_t, sin_t, *, layer, seq, tm):
    tokens = h.shape[0]
    row = pl.BlockSpec((tm, D_MODEL), lambda i: (i, 0))
    tab = pl.BlockSpec((tm, 128), lambda i: (i, 0))
    headed = pl.BlockSpec((HEADS, tm, MLA_QK), lambda i: (0, i, 0))
    wide = jax.ShapeDtypeStruct((tokens, D_MODEL), BF16)
    qk = jax.ShapeDtypeStruct((HEADS, tokens, MLA_QK), BF16)
    return pl.pallas_call(
        functools.partial(_inproj_kernel, layer=layer),
        grid=(tokens // tm,),
        in_specs=[
            row,
            _ada_spec(layer, seq // tm),
            _stacked(w["w_hg"], layer),
            _stacked(w["w_lat"], layer),
            _stacked(w["w_gate"], layer),
            _resident(lbraw.shape),
            _stacked(w["q_norm_g"], layer),
            _stacked(w["w_uq"], layer),
            _stacked(w["kv_norm_g"], layer),
            _stacked(w["w_uk"], layer),
            _stacked(w["w_vt"], layer),
            tab, tab,
        ],
        out_specs=[row, row, row, row, row, headed, headed,
                   pl.BlockSpec((HEADS * MLA_V, tm), lambda i: (0, i)), row, row],
        out_shape=[wide, jax.ShapeDtypeStruct((tokens, D_MODEL), F32), wide, wide, wide,
                   qk, qk, jax.ShapeDtypeStruct((HEADS * MLA_V, tokens), BF16), wide, wide],
        compiler_params=_params("arbitrary"),
        name="inproj",
    )(h, ada, w["w_hg"], w["w_lat"], w["w_gate"], lbraw, w["q_norm_g"], w["w_uq"],
      w["kv_norm_g"], w["w_uk"], w["w_vt"], cos_t, sin_t)


def _hgrn_masks():
    t = np.arange(HG_CHUNK)[:, None]
    s = np.arange(HG_CHUNK)[None, :]
    masks = [(t == s)]
    m = 1
    while m < HG_CHUNK:
        masks.append((t // (2 * m) == s // (2 * m)) & ((t & m) != 0) & ((s & m) == 0))
        m *= 2
    return np.stack(masks).astype(np.float32)


def _hgrn_kernel(q_ref, lf_ref, k_ref, v_ref, mask_ref, o_ref, state_ref, *, n_chunks):
    @pl.when(pl.program_id(1) == 0)
    def _():
        state_ref[...] = jnp.zeros_like(state_ref)

    n_grp = HG_CHUNK // SUBLANES
    sub = lax.broadcasted_iota(jnp.int32, (SUBLANES, HG_DK), 0)
    sign_small = {m: jnp.where((sub & m) != 0, 1.0, -1.0) for m in (1, 2, 4)}

    def bcast_row(x, i):
        return jnp.broadcast_to(x[i:i + 1, :], (SUBLANES, HG_DK))

    def prepare(rows, hh):
        cols = slice(hh * HG_DK, (hh + 1) * HG_DK)
        q_bf = q_ref[rows, cols]
        k_bf = k_ref[rows, cols]
        q = q_bf.astype(F32)
        k = k_bf.astype(F32)

        lf = lf_ref[rows, cols]
        grp = [lf[SUBLANES * r:SUBLANES * (r + 1)] for r in range(n_grp)]
        for sh in (1, 2, 4):
            grp = [p + jnp.where(sub >= sh, pltpu.roll(p, sh, axis=0), 0.0) for p in grp]
        g_grp, ends = [], []
        for r in range(n_grp):
            gp = grp[r] if r == 0 else grp[r] + ends[r - 1]
            g_grp.append(gp)
            ends.append(bcast_row(gp, SUBLANES - 1))
        q_dec = (q * jnp.exp2(jnp.concatenate(g_grp, axis=0))).astype(BF16)
        diag = _dot_nt(q_bf, k_bf) * mask_ref[0]
        return dict(q=q, k=k, g_grp=g_grp, ends=ends, q_dec=q_dec, scores=diag)

    def level(hd, m, lvl):
        g_grp, ends = hd["g_grp"], hd["ends"]
        if m >= SUBLANES:
            gm = m // SUBLANES
            refs = [ends[(r // (2 * gm)) * 2 * gm + gm - 1] for r in range(n_grp)]
            args = [gp - rf if (SUBLANES * r) & m else rf - gp
                    for r, (gp, rf) in enumerate(zip(g_grp, refs))]
        else:
            if m == 4:
                refs = [bcast_row(gp, 3) for gp in g_grp]
            elif m == 2:
                refs = [jnp.where(sub < 4, bcast_row(gp, 1), bcast_row(gp, 5)) for gp in g_grp]
            else:
                refs = [jnp.where((sub & 1) != 0, pltpu.roll(gp, 1, axis=0), gp) for gp in g_grp]
            args = [(gp - rf) * sign_small[m] for gp, rf in zip(g_grp, refs)]
        if m >= SUBLANES:
            zero = jnp.zeros((SUBLANES, HG_DK), F32)
            q_rows, k_rows = [], []
            for r, a in enumerate(args):
                rows_r = slice(SUBLANES * r, SUBLANES * (r + 1))
                e_r = jnp.exp2(a)
                is_query = bool((SUBLANES * r) & m)
                q_rows.append(hd["q"][rows_r] * e_r if is_query else zero)
                k_rows.append(zero if is_query else hd["k"][rows_r] * e_r)
            q_side = jnp.concatenate(q_rows, axis=0)
            k_side = jnp.concatenate(k_rows, axis=0)
        else:
            e = jnp.exp2(jnp.concatenate(args, axis=0))
            q_side = hd["q"] * e
            k_side = hd["k"] * e
        part = _dot_nt(q_side.astype(BF16), k_side.astype(BF16))
        if 2 * m == HG_CHUNK:
            hd["scores"] = hd["scores"] + part
        else:
            hd["scores"] = hd["scores"] + part * mask_ref[lvl]

    def finish(rows, hh, hd):
        cols = slice(hh * HG_DK, (hh + 1) * HG_DK)
        v = v_ref[rows, cols]
        state = state_ref[hh]
        o = _dot_nt(hd["q_dec"], state.astype(BF16)) + _dot(hd["scores"].astype(BF16), v)
        g_end = hd["ends"][n_grp - 1]
        to_end = jnp.exp2(jnp.concatenate([g_end - gp for gp in hd["g_grp"]], axis=0))
        k_dec = (hd["k"] * to_end).astype(BF16)
        state_ref[hh] = jnp.exp2(g_end[0:1, :]) * state + _dot_tn(v, k_dec)
        o_ref[rows, cols] = o.astype(o_ref.dtype)

    def chunk_group(c, carry):
        rows = [pl.ds(pl.multiple_of((HG_CHUNKS_PER_ITER * c + i) * HG_CHUNK, HG_CHUNK), HG_CHUNK)
                for i in range(HG_CHUNKS_PER_ITER)]
        work = [(r, hh, prepare(r, hh)) for r in rows for hh in range(HEADS)]
        m, lvl = 1, 1
        while m < HG_CHUNK:
            for _, _, hd in work:
                level(hd, m, lvl)
            m *= 2
            lvl += 1
        for r, hh, hd in work:
            finish(r, hh, hd)
        return carry

    assert n_chunks % HG_CHUNKS_PER_ITER == 0, n_chunks
    lax.fori_loop(0, n_chunks // HG_CHUNKS_PER_ITER, chunk_group, 0)


def _hgrn_call(hq, lf, hk, hv, *, batch, seq, tb):
    tokens = hq.shape[0]
    per_b = seq // tb
    width = HEADS * HG_DK
    blk = pl.BlockSpec((tb, width), lambda b, i: (b * per_b + i, 0))
    masks = jnp.asarray(_hgrn_masks())
    return pl.pallas_call(
        functools.partial(_hgrn_kernel, n_chunks=tb // HG_CHUNK),
        grid=(batch, per_b),
        in_specs=[blk, blk, blk, blk, _resident(masks.shape)],
        out_specs=blk,
        out_shape=jax.ShapeDtypeStruct((tokens, width), BF16),
        scratch_shapes=[pltpu.VMEM((HEADS, HG_DV, HG_DK), F32)],
        compiler_params=_params("arbitrary", "arbitrary"),
        name="hgrn",
    )(hq, lf, hk, hv, masks)


def _attn_kernel(q_ref, k_ref, vt_ref, o_ref, m_ref, acc_ref, s_ref, bmax_ref, *, tq, hps):
    qi = pl.program_id(2)
    m_ref[...] = jnp.full_like(m_ref, -jnp.inf)
    acc_ref[...] = jnp.zeros_like(acc_ref)
    ones = jnp.ones((ONES_ROWS, tq), BF16)

    def scores(a, j, slot):
        start = pl.multiple_of(j * tq, tq)
        s_t = _dot_nt(k_ref[a, pl.ds(start, tq), :], q_ref[a])
        s_ref[a, slot] = s_t
        bmax_ref[a, slot] = jnp.max(s_t, axis=0, keepdims=True)

    def fold(a, s_t, block_max, v_ext, qcols):
        m_old = m_ref[a, :, qcols]
        m_new = jnp.maximum(m_old, block_max)
        p = jnp.exp2((s_t - m_new).astype(BF16))
        alpha = jnp.exp2(m_old - m_new)
        acc_ref[a, :, qcols] = alpha * acc_ref[a, :, qcols] + _dot(v_ext, p)
        m_ref[a, :, qcols] = m_new

    def v_rows(a, start, n_keys):
        v_t = vt_ref[a * MLA_V:(a + 1) * MLA_V, pl.ds(start, n_keys)]
        return jnp.concatenate([v_t, ones[:, 0:n_keys]], axis=0)

    def accumulate(a, j, slot):
        start = pl.multiple_of(j * tq, tq)
        fold(a, s_ref[a, slot], bmax_ref[a, slot], v_rows(a, start, tq), slice(None))

    def accumulate_diagonal(a, slot):
        start = pl.multiple_of(qi * tq, tq)
        half = tq // 2
        for qcols, n_keys in ((slice(0, half), half), (slice(half, tq), tq)):
            kpos = lax.broadcasted_iota(jnp.int32, (n_keys, half), 0)
            qpos = lax.broadcasted_iota(jnp.int32, (n_keys, half), 1) + qcols.start
            s_t = jnp.where(kpos <= qpos, s_ref[a, slot, 0:n_keys, qcols], MASK_VALUE)
            fold(a, s_t, jnp.max(s_t, axis=0, keepdims=True), v_rows(a, start, n_keys), qcols)

    for a in range(hps):
        scores(a, 0, 0)

    def step(j, slot):
        for a in range(hps):
            scores(a, j + 1, 1 - slot)
            accumulate(a, j, slot)

    def body(jj, carry):
        step(2 * jj, 0)
        step(2 * jj + 1, 1)
        return carry

    lax.fori_loop(0, qi // 2, body, 0)

    @pl.when(qi % 2 == 1)
    def _():
        step(qi - 1, 0)
        for a in range(hps):
            accumulate_diagonal(a, 1)

    @pl.when(qi % 2 == 0)
    def _():
        for a in range(hps):
            accumulate_diagonal(a, 0)

    for a in range(hps):
        acc = acc_ref[a]
        o_t = acc[0:MLA_V] * (1.0 / acc[MLA_V:MLA_V + 1])
        o_ref[:, a * MLA_V:(a + 1) * MLA_V] = o_t.T.astype(o_ref.dtype)


def _attn_call(q, k, vt, *, batch, seq, tq, hps):
    tokens = vt.shape[1]
    nq = seq // tq
    return pl.pallas_call(
        functools.partial(_attn_kernel, tq=tq, hps=hps),
        grid=(batch, HEADS // hps, nq),
        in_specs=[
            pl.BlockSpec((hps, tq, MLA_QK), lambda b, g, i: (g, b * nq + i, 0)),
            pl.BlockSpec((hps, seq, MLA_QK), lambda b, g, i: (g, b, 0)),
            pl.BlockSpec((hps * MLA_V, seq), lambda b, g, i: (g, b)),
        ],
        out_specs=pl.BlockSpec((tq, hps * MLA_V), lambda b, g, i: (b * nq + i, g)),
        out_shape=jax.ShapeDtypeStruct((tokens, HEADS * MLA_V), BF16),
        scratch_shapes=[pltpu.VMEM((hps, 1, tq), F32),
                        pltpu.VMEM((hps, MLA_V + ONES_ROWS, tq), F32),
                        pltpu.VMEM((hps, 2, tq, tq), F32),
                        pltpu.VMEM((hps, 2, 1, tq), F32)],
        compiler_params=_params("arbitrary", "arbitrary", "arbitrary",
                                vmem_limit_bytes=ATTN_VMEM_LIMIT_BYTES),
        name="mla_attn",
    )(q, k, vt)


def _merge_kernel(h_ref, ada_ref, oh_ref, og_ref, gn_ref, om_ref, sga_ref, sgb_ref, wbh_ref, wbm_ref,
                  wo_ref, lng_ref, lnb_ref, o_ref):
    gate = ada_ref[0, 5:6, :]
    parts = _row_parts(h_ref.shape[0], FFN_ROW_GROUPS)

    def hgrn_out(r):
        heads = []
        for hh in range(HEADS):
            cols = slice(hh * HG_DV, (hh + 1) * HG_DV)
            x = _rms_norm(oh_ref[r, cols].astype(F32), gn_ref[:, cols])
            heads.append((x * og_ref[r, cols].astype(F32)).astype(BF16))
        return jnp.concatenate(heads, axis=1)

    y_hg = [_dot(hgrn_out(r), wbh_ref[...]) for r in parts]
    y_mla = [_dot(om_ref[r, :], wbm_ref[...]) for r in parts]
    merged = [(sga_ref[r, :].astype(F32) * a + sgb_ref[r, :].astype(F32) * b).astype(BF16)
              for r, a, b in zip(parts, y_hg, y_mla)]
    y = [_dot(m, wo_ref[...]) for m in merged]
    for r, yy in zip(parts, y):
        res = DN_ALPHA * h_ref[r, :] + (1.0 + gate) * yy
        o_ref[r, :] = _layer_norm(res, lng_ref[...], lnb_ref[...])


def _merge_call(h, ada, oh, og, gnorm, om, sga, sgb, wbh, wbm, wo, ln_g, ln_b, *, layer, seq, tm):
    tokens = h.shape[0]
    row = pl.BlockSpec((tm, D_MODEL), lambda i: (i, 0))
    return pl.pallas_call(
        _merge_kernel,
        grid=(tokens // tm,),
        in_specs=[row, _ada_spec(layer, seq // tm), row, row, _stacked(gnorm, layer), row, row, row,
                  _stacked(wbh, layer), _stacked(wbm, layer), _stacked(wo, layer),
                  _stacked(ln_g, 3 * layer + 1), _stacked(ln_b, 3 * layer + 1)],
        out_specs=row,
        out_shape=jax.ShapeDtypeStruct((tokens, D_MODEL), F32),
        compiler_params=_params("arbitrary"),
        name="merge_out",
    )(h, ada, oh, og, gnorm, om, sga, sgb, wbh, wbm, wo, ln_g, ln_b)


def _mixer_weights(w_in, w_uq, w_ukv, q_norm_g, kv_norm_g):
    n = HEADS * HG_DK
    c_cq = 4 * n
    c_ckv = c_cq + Q_LORA
    c_kr = c_ckv + KV_LORA
    c_ga = c_kr + MLA_ROPE
    half = MLA_ROPE // 2
    w_in = w_in.astype(BF16)
    kr = w_in[:, :, c_kr:c_ga]
    kr_swapped = jnp.concatenate([kr[:, :, half:], kr[:, :, :half]], axis=2)
    pad = jnp.zeros((DEPTH, D_MODEL, 128 - MLA_ROPE), BF16)
    w_lat = jnp.concatenate([w_in[:, :, c_cq:c_kr], kr, pad, kr_swapped, pad], axis=2)

    uq = w_uq.astype(BF16).reshape(DEPTH, Q_LORA, HEADS, MLA_QK)
    uq_nope = uq[..., :MLA_NOPE].reshape(DEPTH, Q_LORA, HEADS * MLA_NOPE)
    uq_rope = uq[..., MLA_NOPE:]
    uq_rope_sw = jnp.concatenate([uq_rope[..., half:], uq_rope[..., :half]], axis=3)
    w_uq_p = jnp.concatenate([uq_nope, uq_rope.reshape(DEPTH, Q_LORA, HEADS * MLA_ROPE),
                              uq_rope_sw.reshape(DEPTH, Q_LORA, HEADS * MLA_ROPE)], axis=2)

    ukv = w_ukv.astype(BF16).reshape(DEPTH, KV_LORA, HEADS, MLA_NOPE + MLA_V)
    w_uk = ukv[..., :MLA_NOPE].reshape(DEPTH, KV_LORA, HEADS * MLA_NOPE)
    w_vt = jnp.swapaxes(ukv[..., MLA_NOPE:].reshape(DEPTH, KV_LORA, HEADS * MLA_V), 1, 2)
    return {
        "w_hg": w_in[:, :, :c_cq],
        "w_lat": w_lat,
        "w_gate": w_in[:, :, c_ga:],
        "w_uq": w_uq_p,
        "w_uk": w_uk,
        "w_vt": w_vt,
        "q_norm_g": q_norm_g.reshape(DEPTH, 1, Q_LORA),
        "kv_norm_g": kv_norm_g.reshape(DEPTH, 1, KV_LORA),
    }


def _tile(seq, want):
    t = min(want, seq)
    assert seq % t == 0, (seq, t)
    return t


def kernel(x, c, positions, ada_w, ada_b, ln_g, ln_b, ffn1_gate, ffn1_up, ffn1_down, w_in,
           hg_lower_bound, hg_norm_g, mla_q_norm_g, mla_w_uq, mla_kv_norm_g, mla_w_ukv,
           w_branch_hg, w_branch_mla, w_out, ffn2_gate, ffn2_up, ffn2_down):
    batch, seq, _ = x.shape
    tokens = batch * seq
    tm_ffn = _tile(seq, 256 * FFN_ROW_GROUPS)
    tm_in = _tile(seq, 512)
    tm_merge = _tile(seq, 256 * FFN_ROW_GROUPS)
    tb_hgrn = _tile(seq, 512)
    tq = _tile(seq, 512)
    attn_heads_per_step = 4

    ada = _ada_call(c, ada_w, ada_b).reshape(DEPTH, batch, N_ADA, D_MODEL)
    cos_t, sin_t = _rope_call(positions)
    lbraw = hg_lower_bound.astype(F32)

    ln_g3 = ln_g.reshape(DEPTH * 3, 1, D_MODEL)
    ln_b3 = ln_b.reshape(DEPTH * 3, 1, D_MODEL)
    ffn1 = [w.astype(BF16) for w in (ffn1_gate, ffn1_up, ffn1_down)]
    ffn2 = [w.astype(BF16) for w in (ffn2_gate, ffn2_up, ffn2_down)]
    w_mix = _mixer_weights(w_in, mla_w_uq, mla_w_ukv, mla_q_norm_g, mla_kv_norm_g)
    w_tail = [w.astype(BF16) for w in (w_branch_hg, w_branch_mla, w_out)]
    gnorm = hg_norm_g.reshape(DEPTH, 1, HEADS * HG_DV)

    h = x.reshape(tokens, D_MODEL)
    for l in range(DEPTH):
        h = _ffn_call(h, ada, *ffn1, ln_g3, ln_b3, layer=l, sub=0, seq=seq, tm=tm_ffn)
        hq, lf, hk, hv, og, q, k, vt, sga, sgb = _inproj_call(
            h, ada, w_mix, lbraw, cos_t, sin_t, layer=l, seq=seq, tm=tm_in)
        oh = _hgrn_call(hq, lf, hk, hv, batch=batch, seq=seq, tb=tb_hgrn)
        om = _attn_call(q, k, vt, batch=batch, seq=seq, tq=tq, hps=attn_heads_per_step)
        h = _merge_call(h, ada, oh, og, gnorm, om, sga, sgb, *w_tail, ln_g3, ln_b3,
                        layer=l, seq=seq, tm=tm_merge)
        h = _ffn_call(h, ada, *ffn2, ln_g3, ln_b3, layer=l, sub=2, seq=seq, tm=tm_ffn)
    return h.reshape(batch, seq, D_MODEL)
```

```python
import functools

import jax
import jax.numpy as jnp
import numpy as np
from jax import lax
from jax.experimental import pallas as pl
from jax.experimental.pallas import tpu as pltpu

F32 = jnp.float32
BF16 = jnp.bfloat16

D_MODEL = 1024
DEPTH = 4
HEADS = 8
HG_DK = 128
HG_DV = 128
HG_CHUNK = 64
LB_FLOOR = 1e-30
MLA_NOPE = 128
MLA_ROPE = 64
MLA_QK = MLA_NOPE + MLA_ROPE
MLA_V = 128
Q_LORA = 384
KV_LORA = 256
ROPE_THETA = 10000.0
MASK_VALUE = -1e30
D_FF = 2816
DN_ALPHA = (2.0 * DEPTH) ** 0.25
LN_EPS = 1e-5
RMS_EPS = 1e-6
N_ADA = 9
LOG2E = 1.4426950408889634
ATTN_SCALE_LOG2E = float(MLA_QK) ** -0.5 * LOG2E
SUBLANES = 8
ONES_ROWS = 16
FFN_ROW_GROUPS = 4
HG_CHUNKS_PER_ITER = 8

VMEM_LIMIT_BYTES = 56 * 1024 * 1024
ATTN_VMEM_LIMIT_BYTES = 63 * 1024 * 1024


def _params(*sem, vmem_limit_bytes=VMEM_LIMIT_BYTES):
    return pltpu.CompilerParams(dimension_semantics=sem, vmem_limit_bytes=vmem_limit_bytes)


def _resident(shape):
    zeros = (0,) * len(shape)
    return pl.BlockSpec(shape, lambda *_: zeros, pipeline_mode=pl.Buffered(1))


def _stacked(arr, index):
    rest = (0,) * (arr.ndim - 1)
    return pl.BlockSpec((None,) + arr.shape[1:], lambda *_: (index,) + rest,
                        pipeline_mode=pl.Buffered(1))


def _silu(x):
    return x * jax.nn.sigmoid(x)


def _dot(a, b):
    return jnp.dot(a, b, preferred_element_type=F32)


def _dot_nt(a, b):
    return lax.dot_general(a, b, (((1,), (1,)), ((), ())), preferred_element_type=F32)


def _dot_tn(a, b):
    return lax.dot_general(a, b, (((0,), (0,)), ((), ())), preferred_element_type=F32)


def _row_parts(rows, n):
    assert rows % n == 0, (rows, n)
    return [slice(i * rows // n, (i + 1) * rows // n) for i in range(n)]


def _layer_norm(r, g, b):
    mu = jnp.mean(r, axis=-1, keepdims=True)
    d = r - mu
    var = jnp.mean(d * d, axis=-1, keepdims=True)
    return d * lax.rsqrt(var + LN_EPS) * g + b


def _rms_norm(x, g):
    return x * lax.rsqrt(jnp.mean(x * x, axis=-1, keepdims=True) + RMS_EPS) * g


def _ada_kernel(c_ref, w_ref, b_ref, o_ref):
    cond = _silu(c_ref[...])
    o_ref[0] = jnp.dot(cond, w_ref[0], preferred_element_type=F32,
                       precision=lax.Precision.HIGHEST) + b_ref[0]


def _ada_call(c, ada_w, ada_b):
    batch = c.shape[0]
    n_out = ada_w.shape[-1]
    tn = D_MODEL
    return pl.pallas_call(
        _ada_kernel,
        grid=(DEPTH, n_out // tn),
        in_specs=[
            pl.BlockSpec((batch, D_MODEL), lambda l, j: (0, 0)),
            pl.BlockSpec((1, D_MODEL, tn), lambda l, j: (l, 0, j)),
            pl.BlockSpec((1, 1, tn), lambda l, j: (l, 0, j)),
        ],
        out_specs=pl.BlockSpec((1, batch, tn), lambda l, j: (l, 0, j)),
        out_shape=jax.ShapeDtypeStruct((DEPTH, batch, n_out), F32),
        compiler_params=_params("arbitrary", "arbitrary"),
        name="ada",
    )(c, ada_w, ada_b.reshape(DEPTH, 1, n_out))


def _rope_kernel(pos_ref, inv_ref, sign_ref, cos_ref, sin_ref):
    ang = pos_ref[...].astype(F32) * inv_ref[...]
    cos_ref[...] = jnp.cos(ang)
    sin_ref[...] = jnp.sin(ang) * sign_ref[...]


def _rope_call(positions):
    tokens = positions.size
    tm = min(1024, tokens)
    half = MLA_ROPE // 2
    inv = 1.0 / (ROPE_THETA ** (jnp.arange(0, MLA_ROPE, 2, dtype=F32) / MLA_ROPE))
    inv_t = jnp.tile(inv, 4).reshape(1, 4 * half)
    sign = jnp.tile(jnp.concatenate([-jnp.ones((half,), F32), jnp.ones((half,), F32)]), 2)
    sign = sign.reshape(1, 4 * half)
    row = pl.BlockSpec((tm, 4 * half), lambda i: (i, 0))
    const = pl.BlockSpec((1, 4 * half), lambda i: (0, 0))
    return pl.pallas_call(
        _rope_kernel,
        grid=(tokens // tm,),
        in_specs=[pl.BlockSpec((tm, 1), lambda i: (i, 0)), const, const],
        out_specs=[row, row],
        out_shape=[jax.ShapeDtypeStruct((tokens, 4 * half), F32)] * 2,
        compiler_params=_params("arbitrary"),
        name="rope_tables",
    )(positions.reshape(tokens, 1), inv_t, sign)


def _ffn_kernel(h_ref, ada_ref, wg_ref, wu_ref, wd_ref, lng_ref, lnb_ref, o_ref, *, k0):
    shift = ada_ref[0, k0:k0 + 1, :]
    scale = ada_ref[0, k0 + 1:k0 + 2, :]
    gate = ada_ref[0, k0 + 2:k0 + 3, :]
    parts = _row_parts(h_ref.shape[0], FFN_ROW_GROUPS)
    u = [(h_ref[r, :] * (1.0 + scale) + shift).astype(BF16) for r in parts]
    act = []
    for uu in u:
        g = _dot(uu, wg_ref[...])
        up = _dot(uu, wu_ref[...])
        act.append((_silu(g) * up).astype(BF16))
    y = [_dot(a, wd_ref[...]) for a in act]
    for r, yy in zip(parts, y):
        res = DN_ALPHA * h_ref[r, :] + (0.5 * (1.0 + gate)) * yy
        o_ref[r, :] = _layer_norm(res, lng_ref[...], lnb_ref[...])


def _ada_spec(layer, per_b):
    return pl.BlockSpec((None, 1, N_ADA, D_MODEL), lambda i: (layer, i // per_b, 0, 0))


def _ffn_call(h, ada, wg, wu, wd, ln_g, ln_b, *, layer, sub, seq, tm):
    tokens = h.shape[0]
    row = pl.BlockSpec((tm, D_MODEL), lambda i: (i, 0))
    return pl.pallas_call(
        functools.partial(_ffn_kernel, k0=3 * sub),
        grid=(tokens // tm,),
        in_specs=[
            row,
            _ada_spec(layer, seq // tm),
            _stacked(wg, layer),
            _stacked(wu, layer),
            _stacked(wd, layer),
            _stacked(ln_g, 3 * layer + sub),
            _stacked(ln_b, 3 * layer + sub),
        ],
        out_specs=row,
        out_shape=jax.ShapeDtypeStruct((tokens, D_MODEL), F32),
        compiler_params=_params("arbitrary"),
        name="ffn",
    )(h, ada, wg, wu, wd, ln_g, ln_b)


def _inproj_kernel(h_ref, ada_ref, whg_ref, wlat_ref, wgate_ref, lbraw_ref, qng_ref, wuq_ref,
                   kvng_ref, wuk_ref, wvt_ref, cos_ref, sin_ref,
                   hq_ref, lf_ref, hk_ref, hv_ref, og_ref, q_ref, kn_ref, kr_ref, vt_ref, sga_ref,
                   sgb_ref,
                   *, layer):
    shift = ada_ref[0, 3:4, :]
    scale = ada_ref[0, 4:5, :]

    raw = lbraw_ref[...]
    e = jnp.exp(raw - jnp.max(raw, axis=0, keepdims=True))
    lb = jnp.zeros((1, HEADS * HG_DK), F32)
    for i in range(1, layer + 1):
        lb = lb + e[i:i + 1, :]
    lb = lb / jnp.sum(e, axis=0, keepdims=True)
    one_minus_lb = 1.0 - lb

    def project(rows):
        u = (h_ref[rows, :] * (1.0 + scale) + shift).astype(BF16)
        n = HEADS * HG_DK
        zf = _dot(u, whg_ref[:, n:2 * n])
        t = jnp.exp(-jnp.abs(zf))
        r = 1.0 / (1.0 + t)
        tr = t * r
        pos = zf >= 0.0
        sig_pos = jnp.where(pos, r, tr)
        sig_neg = jnp.where(pos, tr, r)
        lf_ref[rows, :] = jnp.log2(jnp.maximum(lb, LB_FLOOR) + one_minus_lb * sig_pos)
        hk_ref[rows, :] = (one_minus_lb * sig_neg).astype(BF16)

        zgate = _dot(u, wgate_ref[...])
        sga_ref[rows, :] = jax.nn.sigmoid(zgate[:, 0:D_MODEL]).astype(sga_ref.dtype)
        sgb_ref[rows, :] = jax.nn.sigmoid(zgate[:, D_MODEL:2 * D_MODEL]).astype(sgb_ref.dtype)

        zl = _dot(u, wlat_ref[...])
        cos_t = cos_ref[rows, :]
        sin_t = sin_ref[rows, :]
        cqn = _rms_norm(zl[:, 0:Q_LORA], qng_ref[...]).astype(BF16)
        qf = _dot(cqn, wuq_ref[...]) * ATTN_SCALE_LOG2E
        nq = HEADS * MLA_NOPE
        nr = HEADS * MLA_ROPE
        cos_q = jnp.concatenate([cos_t] * (nr // 128), axis=1)
        sin_q = jnp.concatenate([sin_t] * (nr // 128), axis=1)
        q_rope = qf[:, nq:nq + nr] * cos_q + qf[:, nq + nr:nq + 2 * nr] * sin_q
        ckvn = _rms_norm(zl[:, Q_LORA:Q_LORA + KV_LORA], kvng_ref[...]).astype(BF16)
        k_nope = _dot(ckvn, wuk_ref[...])
        vt_ref[:, rows] = _dot_nt(wvt_ref[...], ckvn).astype(BF16)
        c0 = Q_LORA + KV_LORA
        k_rope = (zl[:, c0:c0 + 128] * cos_t + zl[:, c0 + 128:c0 + 256] * sin_t)[:, 0:MLA_ROPE]
        k_rope = k_rope.astype(BF16)
        for hh in range(HEADS):
            q_ref[hh, rows, 0:MLA_NOPE] = qf[:, hh * MLA_NOPE:(hh + 1) * MLA_NOPE].astype(BF16)
            q_ref[hh, rows, MLA_NOPE:MLA_QK] = (
                q_rope[:, hh * MLA_ROPE:(hh + 1) * MLA_ROPE].astype(BF16))
        kn_ref[rows, :] = k_nope.astype(BF16)
        kr_ref[rows, :] = k_rope

        hq_ref[rows, :] = _silu(_dot(u, whg_ref[:, 0:n])).astype(BF16)
        og_ref[rows, :] = _silu(_dot(u, whg_ref[:, 3 * n:4 * n])).astype(og_ref.dtype)
        hv_ref[rows, :] = _dot(u, whg_ref[:, 2 * n:3 * n]).astype(BF16)

    for rows in _row_parts(h_ref.shape[0], 2):
        project(rows)


def _inproj_call(h, ada, w, lbraw, cos_t, sin_t, *, layer, seq, tm):
    tokens = h.shape[0]
    row = pl.BlockSpec((tm, D_MODEL), lambda i: (i, 0))
    tab = pl.BlockSpec((tm, 128), lambda i: (i, 0))
    headed = pl.BlockSpec((HEADS, tm, MLA_QK), lambda i: (0, i, 0))
    wide = jax.ShapeDtypeStruct((tokens, D_MODEL), BF16)
    qk = jax.ShapeDtypeStruct((HEADS, tokens, MLA_QK), BF16)
    return pl.pallas_call(
        functools.partial(_inproj_kernel, layer=layer),
        grid=(tokens // tm,),
        in_specs=[
            row,
            _ada_spec(layer, seq // tm),
            _stacked(w["w_hg"], layer),
            _stacked(w["w_lat"], layer),
            _stacked(w["w_gate"], layer),
            _resident(lbraw.shape),
            _stacked(w["q_norm_g"], layer),
            _stacked(w["w_uq"], layer),
            _stacked(w["kv_norm_g"], layer),
            _stacked(w["w_uk"], layer),
            _stacked(w["w_vt"], layer),
            tab, tab,
        ],
        out_specs=[row, row, row, row, row, headed, row,
                   pl.BlockSpec((tm, MLA_ROPE), lambda i: (i, 0)),
                   pl.BlockSpec((HEADS * MLA_V, tm), lambda i: (0, i)), row, row],
        out_shape=[wide, jax.ShapeDtypeStruct((tokens, D_MODEL), F32), wide, wide, wide,
                   qk, wide, jax.ShapeDtypeStruct((tokens, MLA_ROPE), BF16),
                   jax.ShapeDtypeStruct((HEADS * MLA_V, tokens), BF16), wide, wide],
        compiler_params=_params("arbitrary"),
        name="inproj",
    )(h, ada, w["w_hg"], w["w_lat"], w["w_gate"], lbraw, w["q_norm_g"], w["w_uq"],
      w["kv_norm_g"], w["w_uk"], w["w_vt"], cos_t, sin_t)


def _hgrn_masks():
    t = np.arange(HG_CHUNK)[:, None]
    s = np.arange(HG_CHUNK)[None, :]
    masks = [(t == s)]
    m = 1
    while m < HG_CHUNK:
        masks.append((t // (2 * m) == s // (2 * m)) & ((t & m) != 0) & ((s & m) == 0))
        m *= 2
    return np.stack(masks).astype(np.float32)


def _hgrn_kernel(q_ref, lf_ref, k_ref, v_ref, mask_ref, o_ref, state_ref, *, n_chunks):
    @pl.when(pl.program_id(1) == 0)
    def _():
        state_ref[...] = jnp.zeros_like(state_ref)

    n_grp = HG_CHUNK // SUBLANES
    sub = lax.broadcasted_iota(jnp.int32, (SUBLANES, HG_DK), 0)
    sign_small = {m: jnp.where((sub & m) != 0, 1.0, -1.0) for m in (1, 2, 4)}

    def bcast_row(x, i):
        return jnp.broadcast_to(x[i:i + 1, :], (SUBLANES, HG_DK))

    def prepare(rows, hh):
        cols = slice(hh * HG_DK, (hh + 1) * HG_DK)
        q_bf = q_ref[rows, cols]
        k_bf = k_ref[rows, cols]
        q = q_bf.astype(F32)
        k = k_bf.astype(F32)

        lf = lf_ref[rows, cols]
        grp = [lf[SUBLANES * r:SUBLANES * (r + 1)] for r in range(n_grp)]
        for sh in (1, 2, 4):
            grp = [p + jnp.where(sub >= sh, pltpu.roll(p, sh, axis=0), 0.0) for p in grp]
        g_grp, ends = [], []
        for r in range(n_grp):
            gp = grp[r] if r == 0 else grp[r] + ends[r - 1]
            g_grp.append(gp)
            ends.append(bcast_row(gp, SUBLANES - 1))
        q_dec = (q * jnp.exp2(jnp.concatenate(g_grp, axis=0))).astype(BF16)
        diag = _dot_nt(q_bf, k_bf) * mask_ref[0]
        return dict(q=q, k=k, g_grp=g_grp, ends=ends, q_dec=q_dec, scores=diag)

    def level(hd, m, lvl):
        g_grp, ends = hd["g_grp"], hd["ends"]
        if m >= SUBLANES:
            gm = m // SUBLANES
            refs = [ends[(r // (2 * gm)) * 2 * gm + gm - 1] for r in range(n_grp)]
            args = [gp - rf if (SUBLANES * r) & m else rf - gp
                    for r, (gp, rf) in enumerate(zip(g_grp, refs))]
        else:
            if m == 4:
                refs = [bcast_row(gp, 3) for gp in g_grp]
            elif m == 2:
                refs = [jnp.where(sub < 4, bcast_row(gp, 1), bcast_row(gp, 5)) for gp in g_grp]
            else:
                refs = [jnp.where((sub & 1) != 0, pltpu.roll(gp, 1, axis=0), gp) for gp in g_grp]
            args = [(gp - rf) * sign_small[m] for gp, rf in zip(g_grp, refs)]
        if m >= SUBLANES:
            zero = jnp.zeros((SUBLANES, HG_DK), F32)
            q_rows, k_rows = [], []
            for r, a in enumerate(args):
                rows_r = slice(SUBLANES * r, SUBLANES * (r + 1))
                e_r = jnp.exp2(a)
                is_query = bool((SUBLANES * r) & m)
                q_rows.append(hd["q"][rows_r] * e_r if is_query else zero)
                k_rows.append(zero if is_query else hd["k"][rows_r] * e_r)
            q_side = jnp.concatenate(q_rows, axis=0)
            k_side = jnp.concatenate(k_rows, axis=0)
        else:
            e = jnp.exp2(jnp.concatenate(args, axis=0))
            q_side = hd["q"] * e
            k_side = hd["k"] * e
        part = _dot_nt(q_side.astype(BF16), k_side.astype(BF16))
        if 2 * m == HG_CHUNK:
            hd["scores"] = hd["scores"] + part
        else:
            hd["scores"] = hd["scores"] + part * mask_ref[lvl]

    def finish(rows, hh, hd):
        cols = slice(hh * HG_DK, (hh + 1) * HG_DK)
        v = v_ref[rows, cols]
        state = state_ref[hh]
        o = _dot_nt(hd["q_dec"], state.astype(BF16)) + _dot(hd["scores"].astype(BF16), v)
        g_end = hd["ends"][n_grp - 1]
        to_end = jnp.exp2(jnp.concatenate([g_end - gp for gp in hd["g_grp"]], axis=0))
        k_dec = (hd["k"] * to_end).astype(BF16)
        state_ref[hh] = jnp.exp2(g_end[0:1, :]) * state + _dot_tn(v, k_dec)
        o_ref[rows, cols] = o.astype(o_ref.dtype)

    def chunk_group(c, carry):
        rows = [pl.ds(pl.multiple_of((HG_CHUNKS_PER_ITER * c + i) * HG_CHUNK, HG_CHUNK), HG_CHUNK)
                for i in range(HG_CHUNKS_PER_ITER)]
        work = [(r, hh, prepare(r, hh)) for r in rows for hh in range(HEADS)]
        m, lvl = 1, 1
        while m < HG_CHUNK:
            for _, _, hd in work:
                level(hd, m, lvl)
            m *= 2
            lvl += 1
        for r, hh, hd in work:
            finish(r, hh, hd)
        return carry

    assert n_chunks % HG_CHUNKS_PER_ITER == 0, n_chunks
    lax.fori_loop(0, n_chunks // HG_CHUNKS_PER_ITER, chunk_group, 0)


def _hgrn_call(hq, lf, hk, hv, *, batch, seq, tb):
    tokens = hq.shape[0]
    per_b = seq // tb
    width = HEADS * HG_DK
    blk = pl.BlockSpec((tb, width), lambda b, i: (b * per_b + i, 0))
    masks = jnp.asarray(_hgrn_masks())
    return pl.pallas_call(
        functools.partial(_hgrn_kernel, n_chunks=tb // HG_CHUNK),
        grid=(batch, per_b),
        in_specs=[blk, blk, blk, blk, _resident(masks.shape)],
        out_specs=blk,
        out_shape=jax.ShapeDtypeStruct((tokens, width), BF16),
        scratch_shapes=[pltpu.VMEM((HEADS, HG_DV, HG_DK), F32)],
        compiler_params=_params("arbitrary", "arbitrary"),
        name="hgrn",
    )(hq, lf, hk, hv, masks)


def _attn_kernel(q_ref, kn_ref, kr_ref, vt_ref, o_ref, m_ref, acc_ref, s_ref, bmax_ref, *, tq, hps):
    qi = pl.program_id(2)
    m_ref[...] = jnp.full_like(m_ref, -jnp.inf)
    acc_ref[...] = jnp.zeros_like(acc_ref)
    ones = jnp.ones((ONES_ROWS, tq), BF16)

    def scores(a, j, slot):
        start = pl.multiple_of(j * tq, tq)
        keys = jnp.concatenate([kn_ref[pl.ds(start, tq), a * MLA_NOPE:(a + 1) * MLA_NOPE],
                                kr_ref[pl.ds(start, tq), :]], axis=1)
        s_t = _dot_nt(keys, q_ref[a])
        s_ref[a, slot] = s_t
        bmax_ref[a, slot] = jnp.max(s_t, axis=0, keepdims=True)

    def fold(a, s_t, block_max, v_ext, qcols):
        m_old = m_ref[a, :, qcols]
        m_new = jnp.maximum(m_old, block_max)
        p = jnp.exp2((s_t - m_new).astype(BF16))
        alpha = jnp.exp2(m_old - m_new)
        acc_ref[a, :, qcols] = alpha * acc_ref[a, :, qcols] + _dot(v_ext, p)
        m_ref[a, :, qcols] = m_new

    def v_rows(a, start, n_keys):
        v_t = vt_ref[a * MLA_V:(a + 1) * MLA_V, pl.ds(start, n_keys)]
        return jnp.concatenate([v_t, ones[:, 0:n_keys]], axis=0)

    def accumulate(a, j, slot):
        start = pl.multiple_of(j * tq, tq)
        fold(a, s_ref[a, slot], bmax_ref[a, slot], v_rows(a, start, tq), slice(None))

    def accumulate_diagonal(a, slot):
        start = pl.multiple_of(qi * tq, tq)
        half = tq // 2
        for qcols, n_keys in ((slice(0, half), half), (slice(half, tq), tq)):
            kpos = lax.broadcasted_iota(jnp.int32, (n_keys, half), 0)
            qpos = lax.broadcasted_iota(jnp.int32, (n_keys, half), 1) + qcols.start
            s_t = jnp.where(kpos <= qpos, s_ref[a, slot, 0:n_keys, qcols], MASK_VALUE)
            fold(a, s_t, jnp.max(s_t, axis=0, keepdims=True), v_rows(a, start, n_keys), qcols)

    for a in range(hps):
        scores(a, 0, 0)

    def step(j, slot):
        for a in range(hps):
            scores(a, j + 1, 1 - slot)
            accumulate(a, j, slot)

    def body(jj, carry):
        step(2 * jj, 0)
        step(2 * jj + 1, 1)
        return carry

    lax.fori_loop(0, qi // 2, body, 0)

    @pl.when(qi % 2 == 1)
    def _():
        step(qi - 1, 0)
        for a in range(hps):
            accumulate_diagonal(a, 1)

    @pl.when(qi % 2 == 0)
    def _():
        for a in range(hps):
            accumulate_diagonal(a, 0)

    for a in range(hps):
        acc = acc_ref[a]
        o_t = acc[0:MLA_V] * (1.0 / acc[MLA_V:MLA_V + 1])
        o_ref[:, a * MLA_V:(a + 1) * MLA_V] = o_t.T.astype(o_ref.dtype)


def _attn_call(q, kn, kr, vt, *, batch, seq, tq, hps):
    tokens = vt.shape[1]
    nq = seq // tq
    return pl.pallas_call(
        functools.partial(_attn_kernel, tq=tq, hps=hps),
        grid=(batch, HEADS // hps, nq),
        in_specs=[
            pl.BlockSpec((hps, tq, MLA_QK), lambda b, g, i: (g, b * nq + i, 0)),
            pl.BlockSpec((seq, hps * MLA_NOPE), lambda b, g, i: (b, g)),
            pl.BlockSpec((seq, MLA_ROPE), lambda b, g, i: (b, 0)),
            pl.BlockSpec((hps * MLA_V, seq), lambda b, g, i: (g, b)),
        ],
        out_specs=pl.BlockSpec((tq, hps * MLA_V), lambda b, g, i: (b * nq + i, g)),
        out_shape=jax.ShapeDtypeStruct((tokens, HEADS * MLA_V), BF16),
        scratch_shapes=[pltpu.VMEM((hps, 1, tq), F32),
                        pltpu.VMEM((hps, MLA_V + ONES_ROWS, tq), F32),
                        pltpu.VMEM((hps, 2, tq, tq), F32),
                        pltpu.VMEM((hps, 2, 1, tq), F32)],
        compiler_params=_params("arbitrary", "arbitrary", "arbitrary",
                                vmem_limit_bytes=ATTN_VMEM_LIMIT_BYTES),
        name="mla_attn",
    )(q, kn, kr, vt)


def _merge_kernel(h_ref, ada_ref, oh_ref, og_ref, gn_ref, om_ref, sga_ref, sgb_ref, wbh_ref, wbm_ref,
                  wo_ref, lng_ref, lnb_ref, o_ref):
    gate = ada_ref[0, 5:6, :]
    parts = _row_parts(h_ref.shape[0], FFN_ROW_GROUPS)

    def hgrn_out(r):
        heads = []
        for hh in range(HEADS):
            cols = slice(hh * HG_DV, (hh + 1) * HG_DV)
            x = _rms_norm(oh_ref[r, cols].astype(F32), gn_ref[:, cols])
            heads.append((x * og_ref[r, cols].astype(F32)).astype(BF16))
        return jnp.concatenate(heads, axis=1)

    y_hg = [_dot(hgrn_out(r), wbh_ref[...]) for r in parts]
    y_mla = [_dot(om_ref[r, :], wbm_ref[...]) for r in parts]
    merged = [(sga_ref[r, :].astype(F32) * a + sgb_ref[r, :].astype(F32) * b).astype(BF16)
              for r, a, b in zip(parts, y_hg, y_mla)]
    y = [_dot(m, wo_ref[...]) for m in merged]
    for r, yy in zip(parts, y):
        res = DN_ALPHA * h_ref[r, :] + (1.0 + gate) * yy
        o_ref[r, :] = _layer_norm(res, lng_ref[...], lnb_ref[...])


def _merge_call(h, ada, oh, og, gnorm, om, sga, sgb, wbh, wbm, wo, ln_g, ln_b, *, layer, seq, tm):
    tokens = h.shape[0]
    row = pl.BlockSpec((tm, D_MODEL), lambda i: (i, 0))
    return pl.pallas_call(
        _merge_kernel,
        grid=(tokens // tm,),
        in_specs=[row, _ada_spec(layer, seq // tm), row, row, _stacked(gnorm, layer), row, row, row,
                  _stacked(wbh, layer), _stacked(wbm, layer), _stacked(wo, layer),
                  _stacked(ln_g, 3 * layer + 1), _stacked(ln_b, 3 * layer + 1)],
        out_specs=row,
        out_shape=jax.ShapeDtypeStruct((tokens, D_MODEL), F32),
        compiler_params=_params("arbitrary"),
        name="merge_out",
    )(h, ada, oh, og, gnorm, om, sga, sgb, wbh, wbm, wo, ln_g, ln_b)


def _mixer_weights(w_in, w_uq, w_ukv, q_norm_g, kv_norm_g):
    n = HEADS * HG_DK
    c_cq = 4 * n
    c_ckv = c_cq + Q_LORA
    c_kr = c_ckv + KV_LORA
    c_ga = c_kr + MLA_ROPE
    half = MLA_ROPE // 2
    w_in = w_in.astype(BF16)
    kr = w_in[:, :, c_kr:c_ga]
    kr_swapped = jnp.concatenate([kr[:, :, half:], kr[:, :, :half]], axis=2)
    pad = jnp.zeros((DEPTH, D_MODEL, 128 - MLA_ROPE), BF16)
    w_lat = jnp.concatenate([w_in[:, :, c_cq:c_kr], kr, pad, kr_swapped, pad], axis=2)

    uq = w_uq.astype(BF16).reshape(DEPTH, Q_LORA, HEADS, MLA_QK)
    uq_nope = uq[..., :MLA_NOPE].reshape(DEPTH, Q_LORA, HEADS * MLA_NOPE)
    uq_rope = uq[..., MLA_NOPE:]
    uq_rope_sw = jnp.concatenate([uq_rope[..., half:], uq_rope[..., :half]], axis=3)
    w_uq_p = jnp.concatenate([uq_nope, uq_rope.reshape(DEPTH, Q_LORA, HEADS * MLA_ROPE),
                              uq_rope_sw.reshape(DEPTH, Q_LORA, HEADS * MLA_ROPE)], axis=2)

    ukv = w_ukv.astype(BF16).reshape(DEPTH, KV_LORA, HEADS, MLA_NOPE + MLA_V)
    w_uk = ukv[..., :MLA_NOPE].reshape(DEPTH, KV_LORA, HEADS * MLA_NOPE)
    w_vt = jnp.swapaxes(ukv[..., MLA_NOPE:].reshape(DEPTH, KV_LORA, HEADS * MLA_V), 1, 2)
    return {
        "w_hg": w_in[:, :, :c_cq],
        "w_lat": w_lat,
        "w_gate": w_in[:, :, c_ga:],
        "w_uq": w_uq_p,
        "w_uk": w_uk,
        "w_vt": w_vt,
        "q_norm_g": q_norm_g.reshape(DEPTH, 1, Q_LORA),
        "kv_norm_g": kv_norm_g.reshape(DEPTH, 1, KV_LORA),
    }


def _tile(seq, want):
    t = min(want, seq)
    assert seq % t == 0, (seq, t)
    return t


def kernel(x, c, positions, ada_w, ada_b, ln_g, ln_b, ffn1_gate, ffn1_up, ffn1_down, w_in,
           hg_lower_bound, hg_norm_g, mla_q_norm_g, mla_w_uq, mla_kv_norm_g, mla_w_ukv,
           w_branch_hg, w_branch_mla, w_out, ffn2_gate, ffn2_up, ffn2_down):
    batch, seq, _ = x.shape
    tokens = batch * seq
    tm_ffn = _tile(seq, 256 * FFN_ROW_GROUPS)
    tm_in = _tile(seq, 512)
    tm_merge = _tile(seq, 256 * FFN_ROW_GROUPS)
    tb_hgrn = _tile(seq, 512)
    tq = _tile(seq, 512)
    attn_heads_per_step = 4

    ada = _ada_call(c, ada_w, ada_b).reshape(DEPTH, batch, N_ADA, D_MODEL)
    cos_t, sin_t = _rope_call(positions)
    lbraw = hg_lower_bound.astype(F32)

    ln_g3 = ln_g.reshape(DEPTH * 3, 1, D_MODEL)
    ln_b3 = ln_b.reshape(DEPTH * 3, 1, D_MODEL)
    ffn1 = [w.astype(BF16) for w in (ffn1_gate, ffn1_up, ffn1_down)]
    ffn2 = [w.astype(BF16) for w in (ffn2_gate, ffn2_up, ffn2_down)]
    w_mix = _mixer_weights(w_in, mla_w_uq, mla_w_ukv, mla_q_norm_g, mla_kv_norm_g)
    w_tail = [w.astype(BF16) for w in (w_branch_hg, w_branch_mla, w_out)]
    gnorm = hg_norm_g.reshape(DEPTH, 1, HEADS * HG_DV)

    h = x.reshape(tokens, D_MODEL)
    for l in range(DEPTH):
        h = _ffn_call(h, ada, *ffn1, ln_g3, ln_b3, layer=l, sub=0, seq=seq, tm=tm_ffn)
        hq, lf, hk, hv, og, q, kn, kr, vt, sga, sgb = _inproj_call(
            h, ada, w_mix, lbraw, cos_t, sin_t, layer=l, seq=seq, tm=tm_in)
        oh = _hgrn_call(hq, lf, hk, hv, batch=batch, seq=seq, tb=tb_hgrn)
        om = _attn_call(q, kn, kr, vt, batch=batch, seq=seq, tq=tq, hps=attn_heads_per_step)
        h = _merge_call(h, ada, oh, og, gnorm, om, sga, sgb, *w_tail, ln_g3, ln_b3,
                        layer=l, seq=seq, tm=tm_merge)
        h = _ffn_call(h, ada, *ffn2, ln_g3, ln_b3, layer=l, sub=2, seq=seq, tm=tm_ffn)
    return h.reshape(batch, seq, D_MODEL)
```
